```python
import jax, jax.numpy as jnp
from jax import lax
import numpy as np

D_MODEL = 1024
BATCH = 8
SEQ = 4096
DEPTH = 4

ALPHA = (2 * DEPTH) ** 0.25
BETA = (8 * DEPTH) ** -0.25
LN_EPS = 1e-5
A_HEADS = 8
A_HEAD_DIM = 64
IDX_HEADS = 8
IDX_DIM = 64
TOPK_MAX = 256
Q_BLOCK = 128
G_HEADS = 4
G_DK = 128
G_DV = 128
G_RANK = 16
G_TAU = 16.0
G_CHUNK = 64
M_INNER = 2 * D_MODEL
M_HEADS = 4
M_DH = M_INNER // M_HEADS
M_CONV = 4
M_QKV_BLOCK = 4
M_CHUNK = 64
D_FF = 4 * D_MODEL

A_WIDTH = A_HEADS * A_HEAD_DIM
G_WIDTH = G_HEADS * G_DV
EVEN_SPLITS = (A_HEADS * A_HEAD_DIM, A_HEAD_DIM, A_HEAD_DIM,
               IDX_HEADS * IDX_DIM, IDX_DIM, IDX_HEADS,
               G_HEADS * G_DK, G_HEADS * G_DK, G_HEADS * G_DV, G_HEADS * G_DV, G_RANK)
EVEN_COLS = sum(EVEN_SPLITS)

kernel_name = "hybrid_dsa_gla_mlstm_deepnorm"


def layer_norm(x, g, b):
    xf = x.astype(jnp.float32)
    mu = jnp.mean(xf, axis=-1, keepdims=True)
    var = jnp.mean(jnp.square(xf - mu), axis=-1, keepdims=True)
    return ((xf - mu) * lax.rsqrt(var + LN_EPS) * g + b).astype(x.dtype)


def head_norm(x, g):
    xf = x.astype(jnp.float32)
    mu = jnp.mean(xf, axis=-1, keepdims=True)
    var = jnp.mean(jnp.square(xf - mu), axis=-1, keepdims=True)
    return (xf - mu) * lax.rsqrt(var + LN_EPS) * g


def to_chunks(t, L):
    B, H, S = t.shape[:3]
    return jnp.moveaxis(t.reshape(B, H, S // L, L, *t.shape[3:]), 2, 0)


def from_chunks(t):
    t = jnp.moveaxis(t, 0, 2)
    B, H, NC, L = t.shape[:4]
    return t.reshape(B, H, NC * L, *t.shape[4:])


def dsa_attention(q, k, v, qi, ki, wi):
    B, S = q.shape[:2]
    topk = min(TOPK_MAX, S // 4)
    nb = S // Q_BLOCK
    key_pos = jnp.arange(S)

    def block(args):
        qb, qib, wib, start = args
        qpos = start + jnp.arange(Q_BLOCK)
        causal = key_pos[None, :] <= qpos[:, None]
        logits = jnp.einsum('bqhd,bsd->bqhs', qib, ki) * IDX_DIM ** -0.5
        score = jnp.einsum('bqh,bqhs->bqs', wib * IDX_HEADS ** -0.5, jax.nn.relu(logits))
        score = jnp.where(causal[None], score.astype(jnp.float32), -jnp.inf)
        _, idx = lax.top_k(score, topk)
        valid = idx <= qpos[None, :, None]
        kg = jax.vmap(lambda kk, ii: kk[ii])(k, idx)
        vg = jax.vmap(lambda vv, ii: vv[ii])(v, idx)
        s = jnp.einsum('bqhd,bqkd->bqhk', qb, kg).astype(jnp.float32) * A_HEAD_DIM ** -0.5
        s = jnp.where(valid[:, :, None, :], s, -jnp.inf)
        p = jax.nn.softmax(s, axis=-1).astype(vg.dtype)
        return jnp.einsum('bqhk,bqkd->bqhd', p, vg)

    def blocks(t):
        return jnp.moveaxis(t.reshape(B, nb, Q_BLOCK, *t.shape[2:]), 1, 0)

    starts = jnp.arange(nb) * Q_BLOCK
    out = lax.map(block, (blocks(q), blocks(qi), blocks(wi), starts))
    return jnp.moveaxis(out, 0, 1).reshape(B, S, A_HEADS * A_HEAD_DIM)


def gla_chunked(q, k, v, log_a):
    B, H = q.shape[:2]
    tri = jnp.tril(jnp.ones((G_CHUNK, G_CHUNK), bool))

    def step(state, inp):
        qc, kc, vc, ac = inp
        bcum = jnp.cumsum(ac, axis=2)
        o_inter = jnp.einsum('bhld,bhde->bhle', qc * jnp.exp(bcum), state)
        diff = bcum[:, :, :, None, :] - bcum[:, :, None, :, :]
        diff = jnp.where(tri[None, None, :, :, None], diff, -jnp.inf)
        att = jnp.einsum('bhtd,bhsd,bhtsd->bhts', qc, kc, jnp.exp(diff))
        o_intra = jnp.einsum('bhts,bhse->bhte', att, vc)
        g = bcum[:, :, -1]
        kd = kc * jnp.exp(g[:, :, None, :] - bcum)
        state = jnp.exp(g)[..., None] * state + jnp.einsum('bhsd,bhse->bhde', kd, vc)
        return state, o_inter + o_intra

    s0 = jnp.zeros((B, H, G_DK, G_DV), jnp.float32)
    _, o = lax.scan(step, s0, (to_chunks(q, G_CHUNK), to_chunks(k, G_CHUNK),
                               to_chunks(v, G_CHUNK), to_chunks(log_a, G_CHUNK)))
    return from_chunks(o)


def mlstm_chunked(q, k, v, li, lf):
    B, H = q.shape[:2]
    tri = jnp.tril(jnp.ones((M_CHUNK, M_CHUNK), bool))

    def step(carry, inp):
        C, n, m = carry
        qc, kc, vc, lic, lfc = inp
        b = jnp.cumsum(lfc, axis=-1)
        D = b[..., :, None] - b[..., None, :] + lic[..., None, :]
        D = jnp.where(tri, D, -jnp.inf)
        inter = b + m[..., None]
        m_t = jnp.maximum(inter, jnp.max(D, axis=-1))
        w_inter = jnp.exp(inter - m_t)
        P = jnp.exp(D - m_t[..., None]) * jnp.einsum('bhtd,bhsd->bhts', qc, kc)
        num = w_inter[..., None] * jnp.einsum('bhtd,bhde->bhte', qc, C) + jnp.einsum('bhts,bhse->bhte', P, vc)
        den = w_inter * jnp.einsum('bhtd,bhd->bht', qc, n) + jnp.sum(P, axis=-1)
        h = num / jnp.maximum(jnp.abs(den), jnp.exp(-m_t))[..., None]
        g = b[..., -1]
        d_end = g[..., None] - b + lic
        m_new = jnp.maximum(g + m, jnp.max(d_end, axis=-1))
        wk = jnp.exp(d_end - m_new[..., None])
        decay = jnp.exp(g + m - m_new)
        C = decay[..., None, None] * C + jnp.einsum('bhs,bhsd,bhse->bhde', wk, kc, vc)
        n = decay[..., None] * n + jnp.einsum('bhs,bhsd->bhd', wk, kc)
        return (C, n, m_new), h

    init = (jnp.zeros((B, H, M_DH, M_DH), jnp.float32), jnp.zeros((B, H, M_DH), jnp.float32),
            jnp.zeros((B, H), jnp.float32))
    _, h = lax.scan(step, init, (to_chunks(q, M_CHUNK), to_chunks(k, M_CHUNK), to_chunks(v, M_CHUNK),
                                 to_chunks(li, M_CHUNK), to_chunks(lf, M_CHUNK)))
    return from_chunks(h)


def causal_conv(x, w):
    return lax.conv_general_dilated(x, w[:, None, :].astype(x.dtype), window_strides=(1,),
                                    padding=[(w.shape[0] - 1, 0)],
                                    dimension_numbers=('NWC', 'WIO', 'NWC'),
                                    feature_group_count=x.shape[-1])


def even_mixer(x, w_in, g_w2, g_b2, g_norm, w_out):
    B, S, _ = x.shape
    split_idx = np.cumsum(EVEN_SPLITS)[:-1].tolist()
    (a_q, a_k, a_v, i_q, i_k, i_w,
     g_q, g_k, g_v, g_r, g_lr) = jnp.split(x @ w_in, split_idx, axis=-1)
    a_out = dsa_attention(a_q.reshape(B, S, A_HEADS, A_HEAD_DIM), a_k, a_v,
                          i_q.reshape(B, S, IDX_HEADS, IDX_DIM), i_k, i_w)
    log_a = jax.nn.log_sigmoid((g_lr @ g_w2 + g_b2).astype(jnp.float32)) / G_TAU

    def heads(t, d):
        return t.reshape(B, S, G_HEADS, d).transpose(0, 2, 1, 3).astype(jnp.float32)

    o = gla_chunked(heads(g_q, G_DK) * G_DK ** -0.5, heads(g_k, G_DK), heads(g_v, G_DV), heads(log_a, G_DK))
    o = head_norm(o.transpose(0, 2, 1, 3), g_norm).reshape(B, S, G_WIDTH).astype(x.dtype)
    o = o * jax.nn.silu(g_r)
    return jnp.concatenate([a_out, o], axis=-1) @ w_out


def odd_mixer(x, w_in, conv_w, conv_b, w_q, w_k, w_v, w_gate, b_gate, skip, norm_g, w_out):
    B, S, _ = x.shape
    x_m, o_pre = jnp.split(x @ w_in, 2, axis=-1)
    x_c = jax.nn.silu(causal_conv(x_m, conv_w) + conv_b)

    def blockdiag(t, w):
        t = t.reshape(B, S, M_INNER // M_QKV_BLOCK, M_QKV_BLOCK)
        return jnp.einsum('bsnd,nde->bsne', t, w).reshape(B, S, M_INNER)

    q = blockdiag(x_c, w_q)
    k = blockdiag(x_c, w_k)
    v = blockdiag(x_m, w_v)
    gates = (jnp.concatenate([q, k, v], axis=-1) @ w_gate + b_gate).astype(jnp.float32)
    i_pre, f_pre = jnp.split(gates.transpose(0, 2, 1), 2, axis=1)

    def heads(t):
        return t.reshape(B, S, M_HEADS, M_DH).transpose(0, 2, 1, 3).astype(jnp.float32)

    h = mlstm_chunked(heads(q), heads(k) * M_DH ** -0.5, heads(v), i_pre, jax.nn.log_sigmoid(f_pre))
    h = h.transpose(0, 2, 1, 3) * jax.nn.sigmoid(o_pre.astype(jnp.float32)).reshape(B, S, M_HEADS, M_DH)
    y = head_norm(h, norm_g).reshape(B, S, M_INNER) + skip * x_c.astype(jnp.float32)
    return y.astype(x.dtype) @ w_out


def sqrelu_mlp(x, w1, w2):
    h = jax.nn.relu(x @ w1)
    return (h * h) @ w2


def setup_inputs(seed: int = 0) -> dict:
    key = jax.random.key(seed)
    ks = iter(jax.random.split(key, 32))

    def nrm(shape, scale):
        return jax.random.normal(next(ks), shape, jnp.float32) * scale

    NE = (DEPTH + 1) // 2
    NO = DEPTH // 2
    nblk = M_INNER // M_QKV_BLOCK
    f_bias = jnp.linspace(3.0, 6.0, M_HEADS, dtype=jnp.float32)[None, :] + nrm((NO, M_HEADS), 0.01)
    i_bias = nrm((NO, M_HEADS), 0.1)
    return {
        "x": nrm((BATCH, SEQ, D_MODEL), 1.0),
        "ev_w_in": nrm((NE, D_MODEL, EVEN_COLS), D_MODEL ** -0.5),
        "ev_g_w2": nrm((NE, G_RANK, G_HEADS * G_DK), G_RANK ** -0.5),
        "ev_g_b2": nrm((NE, G_HEADS * G_DK), 0.1),
        "ev_g_norm": 1.0 + nrm((NE, G_HEADS, G_DV), 0.02),
        "ev_w_out": nrm((NE, A_WIDTH + G_WIDTH, D_MODEL), (A_WIDTH + G_WIDTH) ** -0.5 * BETA),
        "od_w_in": nrm((NO, D_MODEL, 2 * M_INNER), D_MODEL ** -0.5),
        "od_conv_w": nrm((NO, M_CONV, M_INNER), M_CONV ** -0.5),
        "od_conv_b": nrm((NO, M_INNER), 0.02),
        "od_w_q": nrm((NO, nblk, M_QKV_BLOCK, M_QKV_BLOCK), M_QKV_BLOCK ** -0.5),
        "od_w_k": nrm((NO, nblk, M_QKV_BLOCK, M_QKV_BLOCK), M_QKV_BLOCK ** -0.5),
        "od_w_v": nrm((NO, nblk, M_QKV_BLOCK, M_QKV_BLOCK), M_QKV_BLOCK ** -0.5),
        "od_w_gate": nrm((NO, 3 * M_INNER, 2 * M_HEADS), (3 * M_INNER) ** -0.5),
        "od_b_gate": jnp.concatenate([i_bias, f_bias], axis=-1),
        "od_skip": 1.0 + nrm((NO, M_INNER), 0.02),
        "od_norm": 1.0 + nrm((NO, M_HEADS, M_DH), 0.02),
        "od_w_out": nrm((NO, M_INNER, D_MODEL), M_INNER ** -0.5 * BETA),
        "ln1_g": 1.0 + nrm((DEPTH, D_MODEL), 0.02),
        "ln1_b": nrm((DEPTH, D_MODEL), 0.02),
        "ln2_g": 1.0 + nrm((DEPTH, D_MODEL), 0.02),
        "ln2_b": nrm((DEPTH, D_MODEL), 0.02),
        "mlp_w1": nrm((DEPTH, D_MODEL, D_FF), D_MODEL ** -0.5),
        "mlp_w2": nrm((DEPTH, D_FF, D_MODEL), D_FF ** -0.5 * BETA),
    }


def reference(x, ev_w_in, ev_g_w2, ev_g_b2, ev_g_norm, ev_w_out,
              od_w_in, od_conv_w, od_conv_b, od_w_q, od_w_k, od_w_v, od_w_gate, od_b_gate,
              od_skip, od_norm, od_w_out,
              ln1_g, ln1_b, ln2_g, ln2_b, mlp_w1, mlp_w2):
    for l in range(DEPTH):
        j = l // 2
        if l % 2 == 0:
            h = even_mixer(x, ev_w_in[j], ev_g_w2[j], ev_g_b2[j], ev_g_norm[j], ev_w_out[j])
        else:
            h = odd_mixer(x, od_w_in[j], od_conv_w[j], od_conv_b[j], od_w_q[j], od_w_k[j], od_w_v[j],
                          od_w_gate[j], od_b_gate[j], od_skip[j], od_norm[j], od_w_out[j])
        x = layer_norm(ALPHA * x + h, ln1_g[l], ln1_b[l])
        x = layer_norm(ALPHA * x + sqrelu_mlp(x, mlp_w1[l], mlp_w2[l]), ln2_g[l], ln2_b[l])
    return x
```

```python
import functools

import numpy as np
import jax
import jax.numpy as jnp
from jax import lax
from jax.experimental import pallas as pl
from jax.experimental.pallas import tpu as pltpu

F32 = jnp.float32
BF16 = jnp.bfloat16
I32 = jnp.int32

D_MODEL = 1024
DEPTH = 4
ALPHA = (2 * DEPTH) ** 0.25
LN_EPS = 1e-5
A_HEADS = 8
A_HEAD_DIM = 64
IDX_HEADS = 8
IDX_DIM = 64
TOPK_MAX = 256
Q_BLOCK = 128
KEY_CHUNK = 256
G_HEADS = 4
G_DK = 128
G_DV = 128
G_RANK = 16
G_TAU = 16.0
GLA_CHUNK = 64
M_INNER = 2 * D_MODEL
M_HEADS = 4
M_DH = M_INNER // M_HEADS
M_CONV = 4
M_QKV_BLOCK = 4
MLSTM_CHUNK = 256
D_FF = 4 * D_MODEL

LANES = 128
SUBLANES = 8
VMEM_LIMIT = 56 * 1024 * 1024

EV_AQ, EV_IQ, EV_GQ, EV_GK, EV_GV, EV_GR = 0, 1, 2, 3, 4, 5
EV_KV, EV_KIW, EV_GLR = 24, 25, 26
EV_COLS_PADDED = 27 * LANES

INT_MIN = -2147483648
KEY_NEG_INF = INT_MIN - (-8388608)
MASK_BIAS = -2e30
M_INIT = -1e30


def _cparams(sem):
    return pltpu.CompilerParams(dimension_semantics=sem, vmem_limit_bytes=VMEM_LIMIT)


def _layer_norm(z, g, b):
    mu = jnp.mean(z, axis=-1, keepdims=True)
    zc = z - mu
    var = jnp.mean(zc * zc, axis=-1, keepdims=True)
    return zc * lax.rsqrt(var + LN_EPS) * g + b


def _dot(a, b):
    return jnp.dot(a, b, preferred_element_type=F32)


def _dot_nt(a, b):
    return lax.dot_general(a, b, (((1,), (1,)), ((), ())), preferred_element_type=F32)


def _dot_tn(a, b):
    return lax.dot_general(a, b, (((0,), (0,)), ((), ())), preferred_element_type=F32)


def _split_bf16(x):
    hi = x.astype(BF16)
    lo = (x - hi.astype(F32)).astype(BF16)
    return jnp.concatenate([hi, lo], axis=-1)


def _log_sigmoid(x):
    return jnp.minimum(x, 0.0) - jnp.log1p(jnp.exp(-jnp.abs(x)))


def _matmul_kernel(x_ref, w_ref, o_ref):
    o_ref[...] = _dot(x_ref[...], w_ref[...]).astype(o_ref.dtype)


def _matmul(x, w, *, tm, tn, out_dtype):
    t, k = x.shape
    n = w.shape[1]
    return pl.pallas_call(
        _matmul_kernel,
        grid=(t // tm, n // tn),
        in_specs=[pl.BlockSpec((tm, k), lambda i, j: (i, 0)),
                  pl.BlockSpec((k, tn), lambda i, j: (0, j))],
        out_specs=pl.BlockSpec((tm, tn), lambda i, j: (i, j)),
        out_shape=jax.ShapeDtypeStruct((t, n), out_dtype),
        compiler_params=_cparams(("parallel", "arbitrary")),
        name="in_proj",
    )(x, w)


def _proj_ln_kernel(*refs, n_in):
    a_refs, w_refs = refs[:n_in], refs[n_in:2 * n_in]
    xf_ref, g_ref, b_ref, of_ref, ob_ref = refs[2 * n_in:]
    acc = _dot(a_refs[0][...], w_refs[0][...])
    for a_ref, w_ref in zip(a_refs[1:], w_refs[1:]):
        acc = acc + _dot(a_ref[...], w_ref[...])
    y = _layer_norm(ALPHA * xf_ref[...] + acc, g_ref[...], b_ref[...])
    of_ref[...] = y
    ob_ref[...] = y.astype(BF16)


def _proj_ln(acts, weights, xf, g, b, *, tm):
    t, d = xf.shape
    in_specs = [pl.BlockSpec((tm, a.shape[1]), lambda i: (i, 0)) for a in acts]
    in_specs += [pl.BlockSpec(w.shape, lambda i: (0, 0)) for w in weights]
    in_specs += [pl.BlockSpec((tm, d), lambda i: (i, 0)),
                 pl.BlockSpec((1, d), lambda i: (0, 0)),
                 pl.BlockSpec((1, d), lambda i: (0, 0))]
    return pl.pallas_call(
        functools.partial(_proj_ln_kernel, n_in=len(acts)),
        grid=(t // tm,),
        in_specs=in_specs,
        out_specs=[pl.BlockSpec((tm, d), lambda i: (i, 0)),
                   pl.BlockSpec((tm, d), lambda i: (i, 0))],
        out_shape=[jax.ShapeDtypeStruct((t, d), F32), jax.ShapeDtypeStruct((t, d), BF16)],
        compiler_params=_cparams(("parallel",)),
        name="out_proj_ln",
    )(*acts, *weights, xf, g, b)


def _mlp_kernel(xb_ref, xf_ref, w1_ref, w2_ref, g_ref, b_ref, of_ref, ob_ref, *, ff_chunk):
    xb = xb_ref[...]
    acc = None
    for c in range(D_FF // ff_chunk):
        h = _dot(xb, w1_ref[:, c * ff_chunk:(c + 1) * ff_chunk])
        h = jnp.maximum(h, 0.0)
        part = _dot((h * h).astype(BF16), w2_ref[c * ff_chunk:(c + 1) * ff_chunk, :])
        acc = part if acc is None else acc + part
    y = _layer_norm(ALPHA * xf_ref[...] + acc, g_ref[...], b_ref[...])
    of_ref[...] = y
    ob_ref[...] = y.astype(BF16)


def _mlp(xb, xf, w1, w2, g, b, *, tm, ff_chunk=1024):
    t, d = xf.shape
    const = lambda i: (0, 0)
    return pl.pallas_call(
        functools.partial(_mlp_kernel, ff_chunk=ff_chunk),
        grid=(t // tm,),
        in_specs=[pl.BlockSpec((tm, d), lambda i: (i, 0)),
                  pl.BlockSpec((tm, d), lambda i: (i, 0)),
                  pl.BlockSpec(w1.shape, const, pipeline_mode=pl.Buffered(1)),
                  pl.BlockSpec(w2.shape, const, pipeline_mode=pl.Buffered(1)),
                  pl.BlockSpec((1, d), const),
                  pl.BlockSpec((1, d), const)],
        out_specs=[pl.BlockSpec((tm, d), lambda i: (i, 0)),
                   pl.BlockSpec((tm, d), lambda i: (i, 0))],
        out_shape=[jax.ShapeDtypeStruct((t, d), F32), jax.ShapeDtypeStruct((t, d), BF16)],
        compiler_params=_cparams(("parallel",)),
        name="mlp_ln",
    )(xb, xf, w1, w2, g, b)


def _sortable_key(x):
    bits = pltpu.bitcast(x, I32)
    return jnp.where(bits < 0, INT_MIN - bits, bits)


def _twice(x):
    return jnp.concatenate([x, x], axis=1)


def _dsa_kernel(q_ref, qi_ref, kv_ref, kiw_ref, o_ref,
                qi_s, qa_s, wb_s, skey_s, lo_s, cand_s, p_s, m_s, l_s, acc_s, *, nbatch, topk):
    i = pl.program_id(1)
    nch = i // 2 + 1
    row = lax.broadcasted_iota(I32, (Q_BLOCK, KEY_CHUNK), 0)
    col = lax.broadcasted_iota(I32, (Q_BLOCK, KEY_CHUNK), 1)
    nch_max = skey_s.shape[1]

    for g in range(nbatch):
        qi = qi_ref[g]
        q = q_ref[g]
        w = kiw_ref[g, pl.ds(pl.multiple_of(i * Q_BLOCK, Q_BLOCK), Q_BLOCK), :].astype(F32)
        for h in range(IDX_HEADS):
            qi_s[g, h * Q_BLOCK:(h + 1) * Q_BLOCK, :] = qi[:, h * IDX_DIM:(h + 1) * IDX_DIM]
            wh = w[:, IDX_DIM + h:IDX_DIM + h + 1] * IDX_HEADS ** -0.5 * IDX_DIM ** -0.5
            wb_s[g, h] = jnp.broadcast_to(wh, (Q_BLOCK, LANES))
        for h in range(A_HEADS):
            qh = q[:, h * A_HEAD_DIM:(h + 1) * A_HEAD_DIM].astype(F32) * A_HEAD_DIM ** -0.5
            qa_s[g, h * Q_BLOCK:(h + 1) * Q_BLOCK, :] = qh.astype(BF16)
            m_s[g, h] = jnp.full((Q_BLOCK, LANES), M_INIT, F32)
            l_s[g, h] = jnp.zeros((Q_BLOCK, LANES), F32)
            acc_s[g, h] = jnp.zeros((Q_BLOCK, A_HEAD_DIM), F32)

    def score_body(c, carry):
        start = pl.multiple_of(c * KEY_CHUNK, KEY_CHUNK)
        valid = (start + col) <= (i * Q_BLOCK + row)
        for g in range(nbatch):
            ki = kiw_ref[g, pl.ds(start, KEY_CHUNK), :][:, :IDX_DIM]
            lg = _dot_nt(qi_s[g], ki)
            sc = None
            for h in range(IDX_HEADS):
                t = _twice(wb_s[g, h]) * jnp.maximum(lg[h * Q_BLOCK:(h + 1) * Q_BLOCK], 0.0)
                sc = t if sc is None else sc + t
            sc = jnp.where(valid, sc, -jnp.inf)
            skey_s[g, c] = _sortable_key(sc)
        return carry

    lax.fori_loop(0, nch, score_body, 0)

    def count_where(pred):
        def body(c, accs):
            out = []
            for g in range(nbatch):
                hit = jnp.where(pred(g, c, skey_s[g, c]), 1, 0)
                out.append(accs[g] + hit[:, :LANES] + hit[:, LANES:])
            return tuple(out)
        zero = tuple(jnp.zeros((Q_BLOCK, LANES), I32) for _ in range(nbatch))
        accs = lax.fori_loop(0, nch, body, zero)
        return [jnp.sum(a.astype(F32), axis=1, keepdims=True) for a in accs]

    for g in range(nbatch):
        lo_s[g] = jnp.full((Q_BLOCK, LANES), KEY_NEG_INF + 1, I32)

    @pl.when(i * Q_BLOCK >= topk)
    def _search():
        for g in range(nbatch):
            lo_s[g] = jnp.full((Q_BLOCK, LANES), INT_MIN, I32)

        def bit_body(it, carry):
            inc = lax.shift_left(jnp.int32(1), 31 - it)
            for g in range(nbatch):
                cand_s[g] = lo_s[g] + inc
            cnt = count_where(lambda g, c, kk: kk >= _twice(cand_s[g]))
            for g in range(nbatch):
                lo_s[g] = jnp.where(cnt[g] >= topk, cand_s[g], lo_s[g])
            return carry

        lax.fori_loop(0, 32, bit_body, 0)

        cnt_ge = count_where(lambda g, c, kk: kk >= _twice(lo_s[g]))
        worst = cnt_ge[0]
        for g in range(1, nbatch):
            worst = jnp.maximum(worst, cnt_ge[g])

        @pl.when(jnp.max(worst) > topk)
        def _ties():
            cnt_gt = count_where(lambda g, c, kk: kk > _twice(lo_s[g]))
            need = [topk - cnt_gt[g] for g in range(nbatch)]
            for g in range(nbatch):
                cand_s[g] = jnp.zeros((Q_BLOCK, LANES), I32)

            nbits = max(1, int(np.ceil(np.log2(nch_max * KEY_CHUNK))))

            def idx_body(it, carry):
                inc = lax.shift_left(jnp.int32(1), nbits - 1 - it)

                def pred(g, c, kk):
                    idx = c * KEY_CHUNK + col
                    return jnp.logical_and(kk == _twice(lo_s[g]), idx < _twice(cand_s[g] + inc))

                cnt = count_where(pred)
                for g in range(nbatch):
                    cand_s[g] = jnp.where(cnt[g] < need[g], cand_s[g] + inc, cand_s[g])
                return carry

            lax.fori_loop(0, nbits, idx_body, 0)

            def demote_body(c, carry):
                idx = c * KEY_CHUNK + col
                for g in range(nbatch):
                    kk = skey_s[g, c]
                    drop = jnp.logical_and(kk == _twice(lo_s[g]), idx > _twice(cand_s[g]))
                    skey_s[g, c] = jnp.where(drop, INT_MIN, kk)
                return carry

            lax.fori_loop(0, nch, demote_body, 0)

    def attn_body(c, carry):
        start = pl.multiple_of(c * KEY_CHUNK, KEY_CHUNK)
        for g in range(nbatch):
            kv = kv_ref[g, pl.ds(start, KEY_CHUNK), :]
            k = kv[:, :A_HEAD_DIM]
            v = kv[:, A_HEAD_DIM:]
            bias = jnp.where(skey_s[g, c] >= _twice(lo_s[g]), 0.0, MASK_BIAS)
            s = _dot_nt(qa_s[g], k)
            alphas = []
            for h in range(A_HEADS):
                sh = s[h * Q_BLOCK:(h + 1) * Q_BLOCK] + bias
                m_prev = m_s[g, h]
                m_new = jnp.maximum(m_prev, jnp.max(sh, axis=1, keepdims=True))
                p = jnp.exp(sh - _twice(m_new))
                alpha = jnp.exp(m_prev - m_new)
                l_s[g, h] = alpha * l_s[g, h] + jnp.sum(p, axis=1, keepdims=True)
                m_s[g, h] = m_new
                p_s[g, h * Q_BLOCK:(h + 1) * Q_BLOCK, :] = p.astype(BF16)
                alphas.append(alpha)
            pv = _dot(p_s[g], v)
            for h in range(A_HEADS):
                acc_s[g, h] = acc_s[g, h] * alphas[h][:, :A_HEAD_DIM] + pv[h * Q_BLOCK:(h + 1) * Q_BLOCK]
        return carry

    lax.fori_loop(0, nch, attn_body, 0)

    for g in range(nbatch):
        outs = [acc_s[g, h] / l_s[g, h][:, :A_HEAD_DIM] for h in range(A_HEADS)]
        o_ref[g] = jnp.concatenate(outs, axis=1).astype(o_ref.dtype)


def _dsa(proj, *, nbatch):
    b, s, _ = proj.shape
    topk = min(TOPK_MAX, s // 4)
    assert topk % Q_BLOCK == 0 and s % KEY_CHUNK == 0 and b % nbatch == 0
    nb = s // Q_BLOCK
    nch_max = s // KEY_CHUNK
    width = A_HEADS * A_HEAD_DIM
    return pl.pallas_call(
        functools.partial(_dsa_kernel, nbatch=nbatch, topk=topk),
        grid=(b // nbatch, nb),
        in_specs=[pl.BlockSpec((nbatch, Q_BLOCK, width), lambda bi, i: (bi, i, EV_AQ)),
                  pl.BlockSpec((nbatch, Q_BLOCK, width), lambda bi, i: (bi, i, EV_IQ)),
                  pl.BlockSpec((nbatch, s, LANES), lambda bi, i: (bi, 0, EV_KV)),
                  pl.BlockSpec((nbatch, s, LANES), lambda bi, i: (bi, 0, EV_KIW))],
        out_specs=pl.BlockSpec((nbatch, Q_BLOCK, width), lambda bi, i: (bi, i, 0)),
        out_shape=jax.ShapeDtypeStruct((b, s, width), BF16),
        scratch_shapes=[
            pltpu.VMEM((nbatch, IDX_HEADS * Q_BLOCK, IDX_DIM), BF16),
            pltpu.VMEM((nbatch, A_HEADS * Q_BLOCK, A_HEAD_DIM), BF16),
            pltpu.VMEM((nbatch, IDX_HEADS, Q_BLOCK, LANES), F32),
            pltpu.VMEM((nbatch, nch_max, Q_BLOCK, KEY_CHUNK), I32),
            pltpu.VMEM((nbatch, Q_BLOCK, LANES), I32),
            pltpu.VMEM((nbatch, Q_BLOCK, LANES), I32),
            pltpu.VMEM((nbatch, A_HEADS * Q_BLOCK, KEY_CHUNK), BF16),
            pltpu.VMEM((nbatch, A_HEADS, Q_BLOCK, LANES), F32),
            pltpu.VMEM((nbatch, A_HEADS, Q_BLOCK, LANES), F32),
            pltpu.VMEM((nbatch, A_HEADS, Q_BLOCK, A_HEAD_DIM), F32),
        ],
        compiler_params=_cparams(("parallel", "arbitrary")),
        name="dsa",
    )(proj, proj, proj, proj)


def _gla_constants(chunk):
    t = np.arange(chunk)[:, None]
    r = np.arange(chunk)[None, :]
    blocks = [(r <= t), (r > t)]
    masks = [(t == r)]
    n = chunk // 2
    while n >= 1:
        upper = (t // n) % 2 == 1
        mid = (t // n) * n
        blocks.append(upper & (r >= mid) & (r <= t))
        lower = (t // n) % 2 == 0
        end = (t // n + 1) * n - 1
        blocks.append(lower & (r >= t + 1) & (r <= end))
        s = r
        masks.append(upper & ((s // n) % 2 == 0) & (t // (2 * n) == s // (2 * n)))
        n //= 2
    return (np.concatenate(blocks, axis=0).astype(np.float32),
            np.stack(masks, axis=0).astype(np.float32))


def _gla_kernel(q_ref, k_ref, v_ref, r_ref, lr_ref, w2_ref, b2_ref, gn_ref, sums_ref, masks_ref,
                o_ref, state_s, *, chunk):
    nlev = masks_ref.shape[0] - 1

    @pl.when(pl.program_id(1) == 0)
    def _init():
        state_s[...] = jnp.zeros_like(state_s)

    lr = lr_ref[0][:, :G_RANK]
    sums = sums_ref[...]
    outs = []
    for h in range(G_HEADS):
        cs = slice(h * G_DK, (h + 1) * G_DK)
        x = _dot(lr, w2_ref[:, cs]) + b2_ref[:, cs]
        la = _log_sigmoid(x) * (1.0 / G_TAU)
        dec = _dot(sums, _split_bf16(la))
        e = jnp.exp(dec[:, :G_DK] + dec[:, G_DK:])
        qf = q_ref[0][:, cs].astype(F32) * G_DK ** -0.5
        kb = k_ref[0][:, cs]
        kf = kb.astype(F32)
        vb = v_ref[0][:, cs]
        st = state_s[h]

        o = _dot_nt((qf * e[0:chunk]).astype(BF16), st.astype(BF16))
        att = masks_ref[0] * _dot_nt(qf.astype(BF16), kb)
        for lv in range(nlev):
            base = (2 + 2 * lv) * chunk
            qs = (qf * e[base:base + chunk]).astype(BF16)
            ks = (kf * e[base + chunk:base + 2 * chunk]).astype(BF16)
            att = att + masks_ref[lv + 1] * _dot_nt(qs, ks)
        o = o + _dot(att.astype(BF16), vb)

        kd = (kf * e[chunk:2 * chunk]).astype(BF16)
        state_s[h] = st * e[chunk - 1:chunk] + _dot_tn(vb, kd)

        mu = jnp.mean(o, axis=-1, keepdims=True)
        oc = o - mu
        var = jnp.mean(oc * oc, axis=-1, keepdims=True)
        on = oc * lax.rsqrt(var + LN_EPS) * gn_ref[h:h + 1, :]
        rg = r_ref[0][:, cs].astype(F32)
        outs.append(on * (rg * jax.nn.sigmoid(rg)))
    o_ref[0] = jnp.concatenate(outs, axis=1).astype(o_ref.dtype)


def _gla(proj, w2, b2, gnorm, *, chunk=GLA_CHUNK):
    b, s, _ = proj.shape
    width = G_HEADS * G_DK
    sums, masks = _gla_constants(chunk)
    sums = jnp.asarray(sums, BF16)
    masks = jnp.asarray(masks, F32)
    col = lambda blk: pl.BlockSpec((1, chunk, width), lambda bi, c: (bi, c, blk))
    const2 = lambda bi, c: (0, 0)
    return pl.pallas_call(
        functools.partial(_gla_kernel, chunk=chunk),
        grid=(b, s // chunk),
        in_specs=[col(EV_GQ), col(EV_GK), col(EV_GV), col(EV_GR),
                  pl.BlockSpec((1, chunk, LANES), lambda bi, c: (bi, c, EV_GLR)),
                  pl.BlockSpec(w2.shape, const2),
                  pl.BlockSpec(b2.shape, const2),
                  pl.BlockSpec(gnorm.shape, const2),
                  pl.BlockSpec(sums.shape, const2),
                  pl.BlockSpec(masks.shape, lambda bi, c: (0, 0, 0))],
        out_specs=pl.BlockSpec((1, chunk, width), lambda bi, c: (bi, c, 0)),
        out_shape=jax.ShapeDtypeStruct((b, s, width), BF16),
        scratch_shapes=[pltpu.VMEM((G_HEADS, G_DV, G_DK), F32)],
        compiler_params=_cparams(("parallel", "arbitrary")),
        name="gla",
    )(proj, proj, proj, proj, proj, w2, b2, gnorm, sums, masks)


def _mlstm_pre_kernel(x_ref, win_ref, cw_ref, cb_ref, wq_ref, wk_ref, wv_ref, wg_ref, bg_ref,
                      q_ref, k_ref, v_ref, xc_ref, op_ref, gate_ref, ext_s, *, tm):
    @pl.when(pl.program_id(1) == 0)
    def _init():
        ext_s[0:SUBLANES, :] = jnp.zeros((SUBLANES, M_INNER), F32)

    both = _dot(x_ref[0], win_ref[...])
    x_m = both[:, :M_INNER]
    op_ref[0] = both[:, M_INNER:].astype(op_ref.dtype)
    ext_s[SUBLANES:, :] = x_m
    conv = cb_ref[...]
    for j in range(M_CONV):
        off = SUBLANES - (M_CONV - 1) + j
        conv = conv + cw_ref[j:j + 1, :] * ext_s[off:off + tm, :]
    ext_s[0:SUBLANES, :] = x_m[tm - SUBLANES:, :]
    x_c = conv * jax.nn.sigmoid(conv)
    xc_ref[0] = x_c.astype(xc_ref.dtype)
    xcb = x_c.astype(BF16)
    xmb = x_m.astype(BF16)
    ngroups = M_INNER // LANES
    gates = bg_ref[...]
    for src, w_ref, dst, part in ((xcb, wq_ref, q_ref, 0), (xcb, wk_ref, k_ref, 1), (xmb, wv_ref, v_ref, 2)):
        cols = [_dot(src[:, gi * LANES:(gi + 1) * LANES], w_ref[gi]).astype(BF16) for gi in range(ngroups)]
        y = jnp.concatenate(cols, axis=1)
        dst[0] = y
        gates = gates + _dot(y, wg_ref[part * M_INNER:(part + 1) * M_INNER, :])
    gate_ref[0] = gates[:, :2 * M_HEADS]


def _mlstm_pre(xb, win, cw, cb, wq, wk, wv, wg, bg, *, tm):
    b, s, d = xb.shape
    const2 = lambda bi, t: (0, 0)
    const3 = lambda bi, t: (0, 0, 0)
    tile = lambda dt: jax.ShapeDtypeStruct((b, s, M_INNER), dt)
    ospec = pl.BlockSpec((1, tm, M_INNER), lambda bi, t: (bi, t, 0))
    return pl.pallas_call(
        functools.partial(_mlstm_pre_kernel, tm=tm),
        grid=(b, s // tm),
        in_specs=[pl.BlockSpec((1, tm, d), lambda bi, t: (bi, t, 0)),
                  pl.BlockSpec(win.shape, const2, pipeline_mode=pl.Buffered(1)),
                  pl.BlockSpec(cw.shape, const2),
                  pl.BlockSpec(cb.shape, const2),
                  pl.BlockSpec(wq.shape, const3),
                  pl.BlockSpec(wk.shape, const3),
                  pl.BlockSpec(wv.shape, const3),
                  pl.BlockSpec(wg.shape, const2),
                  pl.BlockSpec(bg.shape, const2)],
        out_specs=[ospec, ospec, ospec, ospec, ospec,
                   pl.BlockSpec((1, tm, 2 * M_HEADS), lambda bi, t: (bi, t, 0))],
        out_shape=[tile(BF16), tile(BF16), tile(BF16), tile(BF16), tile(BF16),
                   jax.ShapeDtypeStruct((b, s, 2 * M_HEADS), F32)],
        scratch_shapes=[pltpu.VMEM((tm + SUBLANES, M_INNER), F32)],
        compiler_params=_cparams(("parallel", "arbitrary")),
        name="mlstm_pre",
    )(xb, win, cw, cb, wq, wk, wv, wg, bg)


def _mlstm_kernel(q_ref, k_ref, v_ref, xc_ref, op_ref, gc_ref, gr_ref, skip_ref, ng_ref, tri_ref,
                  y_ref, c_s, n_s, m_s, *, chunk):
    @pl.when(pl.program_id(1) == 0)
    def _init():
        c_s[...] = jnp.zeros_like(c_s)
        n_s[...] = jnp.zeros_like(n_s)
        m_s[...] = jnp.zeros_like(m_s)

    tri = tri_ref[...]
    t_idx = lax.broadcasted_iota(I32, (chunk, chunk), 0)
    s_idx = lax.broadcasted_iota(I32, (chunk, chunk), 1)
    causal = s_idx <= t_idx
    gc = gc_ref[0]
    gr = gr_ref[0]
    outs = []
    for h in range(M_HEADS):
        cs = slice(h * M_DH, (h + 1) * M_DH)
        li_c = gc[:, h:h + 1]
        lf_c = _log_sigmoid(gc[:, M_HEADS + h:M_HEADS + h + 1])
        li_r = gr[h:h + 1, :]
        lf_r = _log_sigmoid(gr[M_HEADS + h:M_HEADS + h + 1, :])
        bc2 = _dot(tri, _split_bf16(jnp.broadcast_to(lf_c, (chunk, LANES))))
        b_c = (bc2[:, :LANES] + bc2[:, LANES:])[:, 0:1]
        lf_r8 = jnp.broadcast_to(lf_r, (SUBLANES, chunk))
        hi = lf_r8.astype(BF16)
        lo = (lf_r8 - hi.astype(F32)).astype(BF16)
        br2 = _dot_nt(jnp.concatenate([hi, lo], axis=0), tri)
        b_r = br2[0:1, :] + br2[SUBLANES:SUBLANES + 1, :]

        m_prev = m_s[h][0:1, 0:1]
        dmat = jnp.where(causal, b_c - b_r + li_r, -jnp.inf)
        inter = b_c + m_prev
        m_t = jnp.maximum(inter, jnp.max(dmat, axis=1, keepdims=True))
        w_inter = jnp.exp(inter - m_t)

        qb = q_ref[0][:, cs]
        kf = k_ref[0][:, cs].astype(F32) * M_DH ** -0.5
        kb = kf.astype(BF16)
        vb = v_ref[0][:, cs]
        c_prev = c_s[h]
        n_prev = n_s[h][0:1, :]

        p = jnp.exp(dmat - m_t) * _dot_nt(qb, kb)
        num = w_inter * _dot(qb, c_prev.astype(BF16)) + _dot(p.astype(BF16), vb)
        qn = jnp.sum(qb.astype(F32) * n_prev, axis=1, keepdims=True)
        den = w_inter * qn + jnp.sum(p, axis=1, keepdims=True)
        hh = num / jnp.maximum(jnp.abs(den), jnp.exp(-m_t))

        g_tot = b_c[chunk - 1:chunk, :]
        d_end = g_tot - b_c + li_c
        m_new = jnp.maximum(g_tot + m_prev, jnp.max(d_end, axis=0, keepdims=True))
        wk = jnp.exp(d_end - m_new)
        decay = jnp.exp(g_tot + m_prev - m_new)
        kw = kf * wk
        c_s[h] = decay * c_prev + _dot_tn(kw.astype(BF16), vb)
        n_s[h] = jnp.broadcast_to(decay * n_prev + jnp.sum(kw, axis=0, keepdims=True), n_s.shape[1:])
        m_s[h] = jnp.broadcast_to(m_new, m_s.shape[1:])

        hg = hh * jax.nn.sigmoid(op_ref[0][:, cs].astype(F32))
        mu = jnp.mean(hg, axis=-1, keepdims=True)
        hc = hg - mu
        var = jnp.mean(hc * hc, axis=-1, keepdims=True)
        hn = hc * lax.rsqrt(var + LN_EPS) * ng_ref[h:h + 1, :]
        outs.append(hn + skip_ref[:, cs] * xc_ref[0][:, cs].astype(F32))
    y_ref[0] = jnp.concatenate(outs, axis=1).astype(y_ref.dtype)


def _mlstm(q, k, v, xc, op, gates, skip, norm_g, *, chunk=MLSTM_CHUNK):
    b, s, _ = q.shape
    gates_t = jnp.swapaxes(gates, 1, 2)
    tri = jnp.asarray(np.tril(np.ones((chunk, chunk), np.float32)), BF16)
    big = pl.BlockSpec((1, chunk, M_INNER), lambda bi, c: (bi, c, 0))
    const2 = lambda bi, c: (0, 0)
    return pl.pallas_call(
        functools.partial(_mlstm_kernel, chunk=chunk),
        grid=(b, s // chunk),
        in_specs=[big, big, big, big, big,
                  pl.BlockSpec((1, chunk, 2 * M_HEADS), lambda bi, c: (bi, c, 0)),
                  pl.BlockSpec((1, 2 * M_HEADS, chunk), lambda bi, c: (bi, 0, c)),
                  pl.BlockSpec(skip.shape, const2),
                  pl.BlockSpec(norm_g.shape, const2),
                  pl.BlockSpec(tri.shape, const2)],
        out_specs=big,
        out_shape=jax.ShapeDtypeStruct((b, s, M_INNER), BF16),
        scratch_shapes=[pltpu.VMEM((M_HEADS, M_DH, M_DH), F32),
                        pltpu.VMEM((M_HEADS, SUBLANES, M_DH), F32),
                        pltpu.VMEM((M_HEADS, SUBLANES, LANES), F32)],
        compiler_params=_cparams(("parallel", "arbitrary")),
        name="mlstm",
    )(q, k, v, xc, op, gates, gates_t, skip, norm_g, tri)


def _even_w_in(w):
    sizes = (512, 64, 64, 512, 64, 8, 512, 512, 512, 512, 16)
    offs = np.concatenate([[0], np.cumsum(sizes)])
    part = lambda j: w[:, offs[j]:offs[j + 1]]
    a_q, a_k, a_v, i_q, i_k, i_w, g_q, g_k, g_v, g_r, g_lr = (part(j) for j in range(11))
    zeros = lambda n: jnp.zeros((w.shape[0], n), w.dtype)
    cols = [a_q, i_q, g_q, g_k, g_v, g_r, a_k, a_v, i_k, i_w, zeros(LANES - 72), g_lr, zeros(LANES - G_RANK)]
    return jnp.concatenate(cols, axis=1).astype(BF16)


def _block_diag(w):
    per = LANES // M_QKV_BLOCK
    wg = w.reshape(-1, per, M_QKV_BLOCK, M_QKV_BLOCK)
    eye = jnp.eye(per, dtype=w.dtype)
    dense = jnp.einsum('gade,ab->gadbe', wg, eye)
    return dense.reshape(-1, LANES, LANES).astype(BF16)


def _token_tile(t):
    for tm in (512, 256, 128):
        if t % tm == 0:
            return tm
    raise ValueError(f"token count {t} is not a multiple of 128")


def kernel(x, ev_w_in, ev_g_w2, ev_g_b2, ev_g_norm, ev_w_out, od_w_in, od_conv_w, od_conv_b, od_w_q, od_w_k, od_w_v, od_w_gate, od_b_gate, od_skip, od_norm, od_w_out, ln1_g, ln1_b, ln2_g, ln2_b, mlp_w1, mlp_w2):
    b, s, d = x.shape
    t = b * s
    tm = _token_tile(t)
    xf = x.reshape(t, d)
    xb = xf.astype(BF16)
    row = lambda v: v.reshape(1, -1)
    a_width = A_HEADS * A_HEAD_DIM
    for l in range(DEPTH):
        j = l // 2
        if l % 2 == 0:
            proj = _matmul(xb, _even_w_in(ev_w_in[j]), tm=tm, tn=EV_COLS_PADDED // 3, out_dtype=BF16)
            proj = proj.reshape(b, s, EV_COLS_PADDED)
            a_out = _dsa(proj, nbatch=2 if b % 2 == 0 else 1)
            g_out = _gla(proj, ev_g_w2[j].astype(BF16), row(ev_g_b2[j]), ev_g_norm[j])
            w_out = ev_w_out[j].astype(BF16)
            acts = [a_out.reshape(t, a_width), g_out.reshape(t, -1)]
            weights = [w_out[:a_width], w_out[a_width:]]
        else:
            wg = jnp.pad(od_w_gate[j], ((0, 0), (0, LANES - 2 * M_HEADS))).astype(BF16)
            bg = jnp.pad(od_b_gate[j], (0, LANES - 2 * M_HEADS)).reshape(1, LANES)
            q, k, v, xc, op, gates = _mlstm_pre(
                xb.reshape(b, s, d), od_w_in[j].astype(BF16), od_conv_w[j], row(od_conv_b[j]),
                _block_diag(od_w_q[j]), _block_diag(od_w_k[j]), _block_diag(od_w_v[j]), wg, bg,
                tm=min(256, s))
            y = _mlstm(q, k, v, xc, op, gates, row(od_skip[j]), od_norm[j], chunk=min(MLSTM_CHUNK, s))
            acts = [y.reshape(t, M_INNER)]
            weights = [od_w_out[j].astype(BF16)]
        xf, xb = _proj_ln(acts, weights, xf, row(ln1_g[l]), row(ln1_b[l]), tm=tm)
        xf, xb = _mlp(xb, xf, mlp_w1[l].astype(BF16), mlp_w2[l].astype(BF16), row(ln2_g[l]), row(ln2_b[l]), tm=tm)
    return xf.reshape(b, s, d)
```

```python
import functools

import numpy as np
import jax
import jax.numpy as jnp
from jax import lax
from jax.experimental import pallas as pl
from jax.experimental.pallas import tpu as pltpu

F32 = jnp.float32
BF16 = jnp.bfloat16
I32 = jnp.int32

D_MODEL = 1024
DEPTH = 4
ALPHA = (2 * DEPTH) ** 0.25
LN_EPS = 1e-5
A_HEADS = 8
A_HEAD_DIM = 64
IDX_HEADS = 8
IDX_DIM = 64
TOPK_MAX = 256
Q_BLOCK = 128
KEY_CHUNK = 256
G_HEADS = 4
G_DK = 128
G_DV = 128
G_RANK = 16
G_TAU = 16.0
GLA_CHUNK = 128
M_INNER = 2 * D_MODEL
M_HEADS = 4
M_DH = M_INNER // M_HEADS
M_CONV = 4
M_QKV_BLOCK = 4
MLSTM_CHUNK = 256
D_FF = 4 * D_MODEL

LANES = 128
SUBLANES = 8
VMEM_LIMIT = 56 * 1024 * 1024

EV_AQ, EV_IQ, EV_GQ, EV_GK, EV_GV, EV_GR = 0, 1, 2, 3, 4, 5
EV_KV, EV_KIW, EV_GLR = 24, 25, 26
EV_COLS_PADDED = 27 * LANES

INT_MIN = -2147483648
KEY_NEG_INF = INT_MIN - (-8388608)
MASK_BIAS = -2e30
M_INIT = -1e30


def _cparams(sem):
    return pltpu.CompilerParams(dimension_semantics=sem, vmem_limit_bytes=VMEM_LIMIT)


def _layer_norm(z, g, b):
    mu = jnp.mean(z, axis=-1, keepdims=True)
    zc = z - mu
    var = jnp.mean(zc * zc, axis=-1, keepdims=True)
    return zc * lax.rsqrt(var + LN_EPS) * g + b


def _dot(a, b):
    return jnp.dot(a, b, preferred_element_type=F32)


def _dot_nt(a, b):
    return lax.dot_general(a, b, (((1,), (1,)), ((), ())), preferred_element_type=F32)


def _dot_tn(a, b):
    return lax.dot_general(a, b, (((0,), (0,)), ((), ())), preferred_element_type=F32)


def _split_bf16(x):
    hi = x.astype(BF16)
    lo = (x - hi.astype(F32)).astype(BF16)
    return jnp.concatenate([hi, lo], axis=-1)


def _log_sigmoid(x):
    return jnp.minimum(x, 0.0) - jnp.log1p(jnp.exp(-jnp.abs(x)))


def _matmul_kernel(x_ref, w_ref, o_ref):
    o_ref[...] = _dot(x_ref[...], w_ref[...]).astype(o_ref.dtype)


def _matmul(x, w, *, tm, tn, out_dtype):
    t, k = x.shape
    n = w.shape[1]
    return pl.pallas_call(
        _matmul_kernel,
        grid=(t // tm, n // tn),
        in_specs=[pl.BlockSpec((tm, k), lambda i, j: (i, 0)),
                  pl.BlockSpec((k, tn), lambda i, j: (0, j))],
        out_specs=pl.BlockSpec((tm, tn), lambda i, j: (i, j)),
        out_shape=jax.ShapeDtypeStruct((t, n), out_dtype),
        compiler_params=_cparams(("parallel", "arbitrary")),
        name="in_proj",
    )(x, w)


def _proj_ln_kernel(*refs, n_in):
    a_refs, w_refs = refs[:n_in], refs[n_in:2 * n_in]
    xf_ref, g_ref, b_ref, of_ref, ob_ref = refs[2 * n_in:]
    acc = _dot(a_refs[0][...], w_refs[0][...])
    for a_ref, w_ref in zip(a_refs[1:], w_refs[1:]):
        acc = acc + _dot(a_ref[...], w_ref[...])
    y = _layer_norm(ALPHA * xf_ref[...] + acc, g_ref[...], b_ref[...])
    of_ref[...] = y
    ob_ref[...] = y.astype(BF16)


def _proj_ln(acts, weights, xf, g, b, *, tm):
    t, d = xf.shape
    in_specs = [pl.BlockSpec((tm, a.shape[1]), lambda i: (i, 0)) for a in acts]
    in_specs += [pl.BlockSpec(w.shape, lambda i: (0, 0)) for w in weights]
    in_specs += [pl.BlockSpec((tm, d), lambda i: (i, 0)),
                 pl.BlockSpec((1, d), lambda i: (0, 0)),
                 pl.BlockSpec((1, d), lambda i: (0, 0))]
    return pl.pallas_call(
        functools.partial(_proj_ln_kernel, n_in=len(acts)),
        grid=(t // tm,),
        in_specs=in_specs,
        out_specs=[pl.BlockSpec((tm, d), lambda i: (i, 0)),
                   pl.BlockSpec((tm, d), lambda i: (i, 0))],
        out_shape=[jax.ShapeDtypeStruct((t, d), F32), jax.ShapeDtypeStruct((t, d), BF16)],
        compiler_params=_cparams(("parallel",)),
        name="out_proj_ln",
    )(*acts, *weights, xf, g, b)


def _mlp_kernel(xb_ref, xf_ref, w1_ref, w2_ref, g_ref, b_ref, of_ref, ob_ref, *, ff_chunk):
    xb = xb_ref[...]
    acc = None
    for c in range(D_FF // ff_chunk):
        h = _dot(xb, w1_ref[:, c * ff_chunk:(c + 1) * ff_chunk])
        h = jnp.maximum(h, 0.0)
        part = _dot((h * h).astype(BF16), w2_ref[c * ff_chunk:(c + 1) * ff_chunk, :])
        acc = part if acc is None else acc + part
    y = _layer_norm(ALPHA * xf_ref[...] + acc, g_ref[...], b_ref[...])
    of_ref[...] = y
    ob_ref[...] = y.astype(BF16)


def _mlp(xb, xf, w1, w2, g, b, *, tm, ff_chunk=1024):
    t, d = xf.shape
    const = lambda i: (0, 0)
    return pl.pallas_call(
        functools.partial(_mlp_kernel, ff_chunk=ff_chunk),
        grid=(t // tm,),
        in_specs=[pl.BlockSpec((tm, d), lambda i: (i, 0)),
                  pl.BlockSpec((tm, d), lambda i: (i, 0)),
                  pl.BlockSpec(w1.shape, const, pipeline_mode=pl.Buffered(1)),
                  pl.BlockSpec(w2.shape, const, pipeline_mode=pl.Buffered(1)),
                  pl.BlockSpec((1, d), const),
                  pl.BlockSpec((1, d), const)],
        out_specs=[pl.BlockSpec((tm, d), lambda i: (i, 0)),
                   pl.BlockSpec((tm, d), lambda i: (i, 0))],
        out_shape=[jax.ShapeDtypeStruct((t, d), F32), jax.ShapeDtypeStruct((t, d), BF16)],
        compiler_params=_cparams(("parallel",)),
        name="mlp_ln",
    )(xb, xf, w1, w2, g, b)


def _sortable_key(x):
    bits = pltpu.bitcast(x, I32)
    return jnp.where(bits < 0, INT_MIN - bits, bits)


def _column_count(hit):
    quarter = jnp.sum(hit.reshape(4, KEY_CHUNK // 4, LANES), axis=0)
    return jnp.sum(quarter.reshape(KEY_CHUNK // 4 // SUBLANES, SUBLANES, LANES), axis=0)


def _dsa_kernel(q_ref, qi_ref, kv_ref, kiw_ref, o_ref,
                qi_s, qa_s, skey_s, skey16_s, thr_s, bias_s, s_s, p_s, m_s, l_s, acc_s, *, nbatch, topk):
    i = pl.program_id(1)
    nch = i // 2 + 1
    key_in_chunk = lax.broadcasted_iota(I32, (KEY_CHUNK, Q_BLOCK), 0)
    q_in_block = lax.broadcasted_iota(I32, (KEY_CHUNK, Q_BLOCK), 1)
    nch_max = skey_s.shape[1]

    sel_r = lax.broadcasted_iota(I32, (2 * SUBLANES, LANES), 0)
    sel_c = lax.broadcasted_iota(I32, (2 * SUBLANES, LANES), 1)
    sel = jnp.where(sel_c == sel_r + IDX_DIM, 1.0, 0.0).astype(BF16)
    w_rows = []
    for g in range(nbatch):
        qi = qi_ref[g]
        q = q_ref[g]
        for h in range(IDX_HEADS):
            qi_s[g, h * Q_BLOCK:(h + 1) * Q_BLOCK, :] = qi[:, h * IDX_DIM:(h + 1) * IDX_DIM]
        for h in range(A_HEADS):
            qh = q[:, h * A_HEAD_DIM:(h + 1) * A_HEAD_DIM].astype(F32) * A_HEAD_DIM ** -0.5
            qa_s[g, h * Q_BLOCK:(h + 1) * Q_BLOCK, :] = qh.astype(BF16)
        w_blk = kiw_ref[g, pl.ds(pl.multiple_of(i * Q_BLOCK, Q_BLOCK), Q_BLOCK), :]
        w_rows.append(_dot_nt(sel, w_blk)[:IDX_HEADS] * (IDX_HEADS ** -0.5 * IDX_DIM ** -0.5))
        m_s[g] = jnp.full(m_s.shape[1:], M_INIT, F32)
        l_s[g] = jnp.zeros(l_s.shape[1:], F32)
        acc_s[g] = jnp.zeros(acc_s.shape[1:], F32)

    npair = (nch + 1) // 2

    def chunk_start(c):
        return pl.multiple_of(jnp.minimum(c, nch - 1) * KEY_CHUNK, KEY_CHUNK)

    def logits_into(c, slot):
        for g in range(nbatch):
            ki = kiw_ref[g, pl.ds(chunk_start(c), KEY_CHUNK), :][:, :IDX_DIM]
            s_s[g, slot] = _dot_nt(ki, qi_s[g])

    def score_from(c, slot):
        valid = (chunk_start(c) + key_in_chunk) <= (i * Q_BLOCK + q_in_block)
        for g in range(nbatch):
            sc = None
            for h in range(IDX_HEADS):
                t = w_rows[g][h:h + 1, :] * jnp.maximum(s_s[g, slot, :, h * Q_BLOCK:(h + 1) * Q_BLOCK], 0.0)
                sc = t if sc is None else sc + t
            key = _sortable_key(jnp.where(valid, sc, -jnp.inf))
            skey_s[g, jnp.minimum(c, nch - 1)] = key
            skey16_s[g, jnp.minimum(c, nch - 1)] = lax.shift_right_arithmetic(key, 16).astype(jnp.int16)

    def score_body(j, carry):
        logits_into(2 * j + 1, 1)
        score_from(2 * j, 0)
        logits_into(2 * j + 2, 0)
        score_from(2 * j + 1, 1)
        return carry

    logits_into(0, 0)
    lax.fori_loop(0, npair, score_body, 0)

    def count_where(pred):
        def body(c, accs):
            return tuple(accs[g] + _column_count(jnp.where(pred(g, c, skey_s[g, c]), 1, 0))
                         for g in range(nbatch))
        zero = tuple(jnp.zeros((SUBLANES, LANES), I32) for _ in range(nbatch))
        accs = lax.fori_loop(0, nch, body, zero)
        return [jnp.sum(a, axis=0, keepdims=True) for a in accs]

    for g in range(nbatch):
        thr_s[g] = jnp.full(thr_s.shape[1:], KEY_NEG_INF + 1, I32)

    def count_high_ge(cands):
        rows = 2 * SUBLANES
        cands16 = [jnp.broadcast_to(cd, (rows, LANES)).astype(jnp.int16) for cd in cands]

        def body(c, accs):
            out = []
            for g in range(nbatch):
                hi = skey16_s[g, c]
                parts = [jnp.where(hi[r * rows:(r + 1) * rows] >= cands16[g], jnp.bfloat16(1), jnp.bfloat16(0))
                         for r in range(KEY_CHUNK // rows)]
                while len(parts) > 1:
                    parts = [parts[a] + parts[a + 1] for a in range(0, len(parts), 2)]
                out.append(accs[g] + parts[0])
            return tuple(out)

        zero = tuple(jnp.zeros((rows, LANES), BF16) for _ in range(nbatch))
        accs = lax.fori_loop(0, nch, body, zero)
        return [jnp.sum(a.astype(F32), axis=0, keepdims=True) for a in accs]

    @pl.when(i * Q_BLOCK >= topk)
    def _search():
        def high_body(it, los):
            inc = lax.shift_left(jnp.int32(1), 15 - it)
            cands = [lo + inc for lo in los]
            cnt = count_high_ge(cands)
            return tuple(jnp.where(cnt[g] >= topk, cands[g], los[g]) for g in range(nbatch))

        highs = lax.fori_loop(0, 16, high_body,
                              tuple(jnp.full((1, LANES), -32768, I32) for _ in range(nbatch)))

        def low_body(it, los):
            inc = lax.shift_left(jnp.int32(1), 15 - it)
            cands = [lo + inc for lo in los]
            cnt = count_where(lambda g, c, kk: kk >= cands[g])
            return tuple(jnp.where(cnt[g] >= topk, cands[g], los[g]) for g in range(nbatch))

        los = lax.fori_loop(0, 16, low_body, tuple(lax.shift_left(hi, 16) for hi in highs))
        for g in range(nbatch):
            thr_s[g] = jnp.broadcast_to(los[g], thr_s.shape[1:])

        cnt_ge = count_where(lambda g, c, kk: kk >= los[g])
        worst = cnt_ge[0]
        for g in range(1, nbatch):
            worst = jnp.maximum(worst, cnt_ge[g])

        @pl.when(jnp.max(worst) > topk)
        def _ties():
            cnt_gt = count_where(lambda g, c, kk: kk > los[g])
            need = [topk - cnt_gt[g] for g in range(nbatch)]
            nbits = max(1, int(np.ceil(np.log2(nch_max * KEY_CHUNK))))

            def idx_body(it, ps):
                inc = lax.shift_left(jnp.int32(1), nbits - 1 - it)

                def pred(g, c, kk):
                    idx = c * KEY_CHUNK + key_in_chunk
                    return jnp.logical_and(kk == los[g], idx < ps[g] + inc)

                cnt = count_where(pred)
                return tuple(jnp.where(cnt[g] < need[g], ps[g] + inc, ps[g]) for g in range(nbatch))

            ps = lax.fori_loop(0, nbits, idx_body,
                               tuple(jnp.zeros((1, LANES), I32) for _ in range(nbatch)))

            def demote_body(c, carry):
                idx = c * KEY_CHUNK + key_in_chunk
                for g in range(nbatch):
                    kk = skey_s[g, c]
                    drop = jnp.logical_and(kk == los[g], idx > ps[g])
                    skey_s[g, c] = jnp.where(drop, INT_MIN, kk)
                return carry

            lax.fori_loop(0, nch, demote_body, 0)

    def qk_into(c, slot):
        for g in range(nbatch):
            k = kv_ref[g, pl.ds(chunk_start(c), KEY_CHUNK), :][:, :A_HEAD_DIM]
            s_s[g, slot] = _dot_nt(k, qa_s[g])

    def softmax_from(c, slot):
        cc = jnp.minimum(c, nch - 1)
        for g in range(nbatch):
            v = kv_ref[g, pl.ds(chunk_start(c), KEY_CHUNK), :][:, A_HEAD_DIM:]
            thr = jnp.where(c < nch, thr_s[g][0:1, :], jnp.int32(2147483647))
            bias_s[g, slot] = jnp.where(skey_s[g, cc] >= thr, 0.0, MASK_BIAS)
            m_prev = m_s[g][0:1, :]
            ms, sums = [], []
            for h in range(A_HEADS):
                cs = slice(h * Q_BLOCK, (h + 1) * Q_BLOCK)
                sh = s_s[g, slot, :, cs] + bias_s[g, slot]
                m_new = jnp.maximum(m_prev[:, cs], jnp.max(sh, axis=0, keepdims=True))
                p = jnp.exp(sh - m_new)
                p_s[g, slot, :, cs] = p.astype(BF16)
                sums.append(jnp.sum(p, axis=0, keepdims=True))
                ms.append(m_new)
            m_new = jnp.concatenate(ms, axis=1)
            alpha = jnp.exp(m_prev - m_new)
            l_s[g] = jnp.broadcast_to(alpha * l_s[g][0:1, :] + jnp.concatenate(sums, axis=1), l_s.shape[1:])
            m_s[g] = jnp.broadcast_to(m_new, m_s.shape[1:])
            acc_s[g] = acc_s[g] * alpha + _dot_tn(v, p_s[g, slot])

    def attn_body(j, carry):
        qk_into(2 * j + 1, 1)
        softmax_from(2 * j, 0)
        qk_into(2 * j + 2, 0)
        softmax_from(2 * j + 1, 1)
        return carry

    qk_into(0, 0)
    lax.fori_loop(0, npair, attn_body, 0)

    for g in range(nbatch):
        out_t = acc_s[g] / l_s[g][0:1, :]
        outs = [out_t[:, h * Q_BLOCK:(h + 1) * Q_BLOCK].T for h in range(A_HEADS)]
        o_ref[g] = jnp.concatenate(outs, axis=1).astype(o_ref.dtype)


def _dsa(proj, *, nbatch):
    b, s, _ = proj.shape
    topk = min(TOPK_MAX, s // 4)
    assert topk % Q_BLOCK == 0 and s % KEY_CHUNK == 0 and b % nbatch == 0
    nb = s // Q_BLOCK
    nch_max = s // KEY_CHUNK
    width = A_HEADS * A_HEAD_DIM
    return pl.pallas_call(
        functools.partial(_dsa_kernel, nbatch=nbatch, topk=topk),
        grid=(b // nbatch, nb),
        in_specs=[pl.BlockSpec((nbatch, Q_BLOCK, width), lambda bi, i: (bi, i, EV_AQ)),
                  pl.BlockSpec((nbatch, Q_BLOCK, width), lambda bi, i: (bi, i, EV_IQ)),
                  pl.BlockSpec((nbatch, s, LANES), lambda bi, i: (bi, 0, EV_KV)),
                  pl.BlockSpec((nbatch, s, LANES), lambda bi, i: (bi, 0, EV_KIW))],
        out_specs=pl.BlockSpec((nbatch, Q_BLOCK, width), lambda bi, i: (bi, i, 0)),
        out_shape=jax.ShapeDtypeStruct((b, s, width), BF16),
        scratch_shapes=[
            pltpu.VMEM((nbatch, IDX_HEADS * Q_BLOCK, IDX_DIM), BF16),
            pltpu.VMEM((nbatch, A_HEADS * Q_BLOCK, A_HEAD_DIM), BF16),
            pltpu.VMEM((nbatch, nch_max, KEY_CHUNK, Q_BLOCK), I32),
            pltpu.VMEM((nbatch, nch_max, KEY_CHUNK, Q_BLOCK), jnp.int16),
            pltpu.VMEM((nbatch, SUBLANES, Q_BLOCK), I32),
            pltpu.VMEM((nbatch, 2, KEY_CHUNK, Q_BLOCK), F32),
            pltpu.VMEM((nbatch, 2, KEY_CHUNK, A_HEADS * Q_BLOCK), F32),
            pltpu.VMEM((nbatch, 2, KEY_CHUNK, A_HEADS * Q_BLOCK), BF16),
            pltpu.VMEM((nbatch, SUBLANES, A_HEADS * Q_BLOCK), F32),
            pltpu.VMEM((nbatch, SUBLANES, A_HEADS * Q_BLOCK), F32),
            pltpu.VMEM((nbatch, A_HEAD_DIM, A_HEADS * Q_BLOCK), F32),
        ],
        compiler_params=_cparams(("parallel", "arbitrary")),
        name="dsa",
    )(proj, proj, proj, proj)


def _gla_constants(chunk):
    t = np.arange(chunk)[:, None]
    r = np.arange(chunk)[None, :]
    blocks = [(r <= t), (r > t)]
    masks = [(t == r)]
    n = chunk // 2
    while n >= 1:
        upper = (t // n) % 2 == 1
        mid = (t // n) * n
        end = (t // n + 1) * n - 1
        blocks.append((upper & (r >= mid) & (r <= t)) | (~upper & (r >= t + 1) & (r <= end)))
        s = r
        masks.append(upper & ((s // n) % 2 == 0) & (t // (2 * n) == s // (2 * n)))
        n //= 2
    return (np.concatenate(blocks, axis=0).astype(np.float32),
            np.stack(masks, axis=0).astype(np.float32))


def _gla_kernel(q_ref, k_ref, v_ref, r_ref, lr_ref, w2_ref, b2_ref, gn_ref, sums_ref, masks_ref,
                o_ref, state_s, *, chunk):
    nlev = masks_ref.shape[0] - 1

    @pl.when(pl.program_id(1) == 0)
    def _init():
        state_s[...] = jnp.zeros_like(state_s)

    lr = lr_ref[0][:, :G_RANK]
    sums = sums_ref[...]
    outs = []
    for h in range(G_HEADS):
        cs = slice(h * G_DK, (h + 1) * G_DK)
        x = _dot(lr, w2_ref[:, cs]) + b2_ref[:, cs]
        la = _log_sigmoid(x) * (1.0 / G_TAU)
        dec = _dot(sums, _split_bf16(la))
        e = jnp.exp(dec[:, :G_DK] + dec[:, G_DK:])
        qf = q_ref[0][:, cs].astype(F32) * G_DK ** -0.5
        kb = k_ref[0][:, cs]
        kf = kb.astype(F32)
        vb = v_ref[0][:, cs]
        st = state_s[h]

        o = _dot_nt((qf * e[0:chunk]).astype(BF16), st.astype(BF16))
        att = masks_ref[0] * _dot_nt(qf.astype(BF16), kb)
        for lv in range(nlev):
            e_lv = e[(2 + lv) * chunk:(3 + lv) * chunk]
            att = att + masks_ref[lv + 1] * _dot_nt((qf * e_lv).astype(BF16), (kf * e_lv).astype(BF16))
        o = o + _dot(att.astype(BF16), vb)

        kd = (kf * e[chunk:2 * chunk]).astype(BF16)
        state_s[h] = st * e[chunk - 1:chunk] + _dot_tn(vb, kd)

        mu = jnp.mean(o, axis=-1, keepdims=True)
        oc = o - mu
        var = jnp.mean(oc * oc, axis=-1, keepdims=True)
        on = oc * lax.rsqrt(var + LN_EPS) * gn_ref[h:h + 1, :]
        rg = r_ref[0][:, cs].astype(F32)
        outs.append(on * (rg * jax.nn.sigmoid(rg)))
    o_ref[0] = jnp.concatenate(outs, axis=1).astype(o_ref.dtype)


def _gla(proj, w2, b2, gnorm, *, chunk=GLA_CHUNK):
    b, s, _ = proj.shape
    width = G_HEADS * G_DK
    sums, masks = _gla_constants(chunk)
    sums = jnp.asarray(sums, BF16)
    masks = jnp.asarray(masks, F32)
    col = lambda blk: pl.BlockSpec((1, chunk, width), lambda bi, c: (bi, c, blk))
    const2 = lambda bi, c: (0, 0)
    return pl.pallas_call(
        functools.partial(_gla_kernel, chunk=chunk),
        grid=(b, s // chunk),
        in_specs=[col(EV_GQ), col(EV_GK), col(EV_GV), col(EV_GR),
                  pl.BlockSpec((1, chunk, LANES), lambda bi, c: (bi, c, EV_GLR)),
                  pl.BlockSpec(w2.shape, const2),
                  pl.BlockSpec(b2.shape, const2),
                  pl.BlockSpec(gnorm.shape, const2),
                  pl.BlockSpec(sums.shape, const2),
                  pl.BlockSpec(masks.shape, lambda bi, c: (0, 0, 0))],
        out_specs=pl.BlockSpec((1, chunk, width), lambda bi, c: (bi, c, 0)),
        out_shape=jax.ShapeDtypeStruct((b, s, width), BF16),
        scratch_shapes=[pltpu.VMEM((G_HEADS, G_DV, G_DK), F32)],
        compiler_params=_cparams(("parallel", "arbitrary")),
        name="gla",
    )(proj, proj, proj, proj, proj, w2, b2, gnorm, sums, masks)


def _mlstm_pre_kernel(x_ref, win_ref, cw_ref, cb_ref, wq_ref, wk_ref, wv_ref, wg_ref, bg_ref,
                      q_ref, k_ref, v_ref, xc_ref, op_ref, gate_ref, ext_s, *, tm):
    @pl.when(pl.program_id(1) == 0)
    def _init():
        ext_s[0:SUBLANES, :] = jnp.zeros((SUBLANES, M_INNER), F32)

    both = _dot(x_ref[0], win_ref[...])
    x_m = both[:, :M_INNER]
    op_ref[0] = both[:, M_INNER:].astype(op_ref.dtype)
    ext_s[SUBLANES:, :] = x_m
    conv = cb_ref[...]
    for j in range(M_CONV):
        off = SUBLANES - (M_CONV - 1) + j
        conv = conv + cw_ref[j:j + 1, :] * ext_s[off:off + tm, :]
    ext_s[0:SUBLANES, :] = x_m[tm - SUBLANES:, :]
    x_c = conv * jax.nn.sigmoid(conv)
    xc_ref[0] = x_c.astype(xc_ref.dtype)
    xcb = x_c.astype(BF16)
    xmb = x_m.astype(BF16)
    ngroups = M_INNER // LANES
    gates = bg_ref[...]
    for src, w_ref, dst, part in ((xcb, wq_ref, q_ref, 0), (xcb, wk_ref, k_ref, 1), (xmb, wv_ref, v_ref, 2)):
        cols = [_dot(src[:, gi * LANES:(gi + 1) * LANES], w_ref[gi]).astype(BF16) for gi in range(ngroups)]
        y = jnp.concatenate(cols, axis=1)
        dst[0] = y
        gates = gates + _dot(y, wg_ref[part * M_INNER:(part + 1) * M_INNER, :])
    gate_ref[0] = gates[:, :2 * M_HEADS]


def _mlstm_pre(xb, win, cw, cb, wq, wk, wv, wg, bg, *, tm):
    b, s, d = xb.shape
    const2 = lambda bi, t: (0, 0)
    const3 = lambda bi, t: (0, 0, 0)
    tile = lambda dt: jax.ShapeDtypeStruct((b, s, M_INNER), dt)
    ospec = pl.BlockSpec((1, tm, M_INNER), lambda bi, t: (bi, t, 0))
    return pl.pallas_call(
        functools.partial(_mlstm_pre_kernel, tm=tm),
        grid=(b, s // tm),
        in_specs=[pl.BlockSpec((1, tm, d), lambda bi, t: (bi, t, 0)),
                  pl.BlockSpec(win.shape, const2, pipeline_mode=pl.Buffered(1)),
                  pl.BlockSpec(cw.shape, const2),
                  pl.BlockSpec(cb.shape, const2),
                  pl.BlockSpec(wq.shape, const3),
                  pl.BlockSpec(wk.shape, const3),
                  pl.BlockSpec(wv.shape, const3),
                  pl.BlockSpec(wg.shape, const2),
                  pl.BlockSpec(bg.shape, const2)],
        out_specs=[ospec, ospec, ospec, ospec, ospec,
                   pl.BlockSpec((1, tm, 2 * M_HEADS), lambda bi, t: (bi, t, 0))],
        out_shape=[tile(BF16), tile(BF16), tile(BF16), tile(BF16), tile(BF16),
                   jax.ShapeDtypeStruct((b, s, 2 * M_HEADS), F32)],
        scratch_shapes=[pltpu.VMEM((tm + SUBLANES, M_INNER), F32)],
        compiler_params=_cparams(("parallel", "arbitrary")),
        name="mlstm_pre",
    )(xb, win, cw, cb, wq, wk, wv, wg, bg)


def _mlstm_kernel(q_ref, k_ref, v_ref, xc_ref, op_ref, gc_ref, gr_ref, skip_ref, ng_ref, tri_ref,
                  y_ref, c_s, n_s, m_s, *, chunk):
    @pl.when(pl.program_id(1) == 0)
    def _init():
        c_s[...] = jnp.zeros_like(c_s)
        n_s[...] = jnp.zeros_like(n_s)
        m_s[...] = jnp.zeros_like(m_s)

    tri = tri_ref[...]
    t_idx = lax.broadcasted_iota(I32, (chunk, chunk), 0)
    s_idx = lax.broadcasted_iota(I32, (chunk, chunk), 1)
    causal = s_idx <= t_idx
    gc = gc_ref[0]
    gr = gr_ref[0]
    outs = []
    for h in range(M_HEADS):
        cs = slice(h * M_DH, (h + 1) * M_DH)
        li_c = gc[:, h:h + 1]
        lf_c = _log_sigmoid(gc[:, M_HEADS + h:M_HEADS + h + 1])
        li_r = gr[h:h + 1, :]
        lf_r = _log_sigmoid(gr[M_HEADS + h:M_HEADS + h + 1, :])
        bc2 = _dot(tri, _split_bf16(jnp.broadcast_to(lf_c, (chunk, LANES))))
        b_c = (bc2[:, :LANES] + bc2[:, LANES:])[:, 0:1]
        lf_r8 = jnp.broadcast_to(lf_r, (SUBLANES, chunk))
        hi = lf_r8.astype(BF16)
        lo = (lf_r8 - hi.astype(F32)).astype(BF16)
        br2 = _dot_nt(jnp.concatenate([hi, lo], axis=0), tri)
        b_r = br2[0:1, :] + br2[SUBLANES:SUBLANES + 1, :]

        m_prev = m_s[h][0:1, 0:1]
        dmat = jnp.where(causal, b_c - b_r + li_r, -jnp.inf)
        inter = b_c + m_prev
        m_t = jnp.maximum(inter, jnp.max(dmat, axis=1, keepdims=True))
        w_inter = jnp.exp(inter - m_t)

        qb = q_ref[0][:, cs]
        kf = k_ref[0][:, cs].astype(F32) * M_DH ** -0.5
        kb = kf.astype(BF16)
        vb = v_ref[0][:, cs]
        c_prev = c_s[h]
        n_prev = n_s[h][0:1, :]

        p = jnp.exp(dmat - m_t) * _dot_nt(qb, kb)
        num = w_inter * _dot(qb, c_prev.astype(BF16)) + _dot(p.astype(BF16), vb)
        qn = jnp.sum(qb.astype(F32) * n_prev, axis=1, keepdims=True)
        den = w_inter * qn + jnp.sum(p, axis=1, keepdims=True)
        hh = num / jnp.maximum(jnp.abs(den), jnp.exp(-m_t))

        g_tot = b_c[chunk - 1:chunk, :]
        d_end = g_tot - b_c + li_c
        m_new = jnp.maximum(g_tot + m_prev, jnp.max(d_end, axis=0, keepdims=True))
        wk = jnp.exp(d_end - m_new)
        decay = jnp.exp(g_tot + m_prev - m_new)
        kw = kf * wk
        c_s[h] = decay * c_prev + _dot_tn(kw.astype(BF16), vb)
        n_s[h] = jnp.broadcast_to(decay * n_prev + jnp.sum(kw, axis=0, keepdims=True), n_s.shape[1:])
        m_s[h] = jnp.broadcast_to(m_new, m_s.shape[1:])

        hg = hh * jax.nn.sigmoid(op_ref[0][:, cs].astype(F32))
        mu = jnp.mean(hg, axis=-1, keepdims=True)
        hc = hg - mu
        var = jnp.mean(hc * hc, axis=-1, keepdims=True)
        hn = hc * lax.rsqrt(var + LN_EPS) * ng_ref[h:h + 1, :]
        outs.append(hn + skip_ref[:, cs] * xc_ref[0][:, cs].astype(F32))
    y_ref[0] = jnp.concatenate(outs, axis=1).astype(y_ref.dtype)


def _mlstm(q, k, v, xc, op, gates, skip, norm_g, *, chunk=MLSTM_CHUNK):
    b, s, _ = q.shape
    gates_t = jnp.swapaxes(gates, 1, 2)
    tri = jnp.asarray(np.tril(np.ones((chunk, chunk), np.float32)), BF16)
    big = pl.BlockSpec((1, chunk, M_INNER), lambda bi, c: (bi, c, 0))
    const2 = lambda bi, c: (0, 0)
    return pl.pallas_call(
        functools.partial(_mlstm_kernel, chunk=chunk),
        grid=(b, s // chunk),
        in_specs=[big, big, big, big, big,
                  pl.BlockSpec((1, chunk, 2 * M_HEADS), lambda bi, c: (bi, c, 0)),
                  pl.BlockSpec((1, 2 * M_HEADS, chunk), lambda bi, c: (bi, 0, c)),
                  pl.BlockSpec(skip.shape, const2),
                  pl.BlockSpec(norm_g.shape, const2),
                  pl.BlockSpec(tri.shape, const2)],
        out_specs=big,
        out_shape=jax.ShapeDtypeStruct((b, s, M_INNER), BF16),
        scratch_shapes=[pltpu.VMEM((M_HEADS, M_DH, M_DH), F32),
                        pltpu.VMEM((M_HEADS, SUBLANES, M_DH), F32),
                        pltpu.VMEM((M_HEADS, SUBLANES, LANES), F32)],
        compiler_params=_cparams(("parallel", "arbitrary")),
        name="mlstm",
    )(q, k, v, xc, op, gates, gates_t, skip, norm_g, tri)


def _even_w_in(w):
    sizes = (512, 64, 64, 512, 64, 8, 512, 512, 512, 512, 16)
    offs = np.concatenate([[0], np.cumsum(sizes)])
    part = lambda j: w[:, offs[j]:offs[j + 1]]
    a_q, a_k, a_v, i_q, i_k, i_w, g_q, g_k, g_v, g_r, g_lr = (part(j) for j in range(11))
    zeros = lambda n: jnp.zeros((w.shape[0], n), w.dtype)
    cols = [a_q, i_q, g_q, g_k, g_v, g_r, a_k, a_v, i_k, i_w, zeros(LANES - 72), g_lr, zeros(LANES - G_RANK)]
    return jnp.concatenate(cols, axis=1).astype(BF16)


def _block_diag(w):
    per = LANES // M_QKV_BLOCK
    wg = w.reshape(-1, per, M_QKV_BLOCK, M_QKV_BLOCK)
    eye = jnp.eye(per, dtype=w.dtype)
    dense = jnp.einsum('gade,ab->gadbe', wg, eye)
    return dense.reshape(-1, LANES, LANES).astype(BF16)


def _token_tile(t):
    for tm in (512, 256, 128):
        if t % tm == 0:
            return tm
    raise ValueError(f"token count {t} is not a multiple of 128")


def kernel(x, ev_w_in, ev_g_w2, ev_g_b2, ev_g_norm, ev_w_out, od_w_in, od_conv_w, od_conv_b, od_w_q, od_w_k, od_w_v, od_w_gate, od_b_gate, od_skip, od_norm, od_w_out, ln1_g, ln1_b, ln2_g, ln2_b, mlp_w1, mlp_w2):
    b, s, d = x.shape
    t = b * s
    tm = _token_tile(t)
    xf = x.reshape(t, d)
    xb = xf.astype(BF16)
    row = lambda v: v.reshape(1, -1)
    a_width = A_HEADS * A_HEAD_DIM
    for l in range(DEPTH):
        j = l // 2
        if l % 2 == 0:
            proj = _matmul(xb, _even_w_in(ev_w_in[j]), tm=tm, tn=EV_COLS_PADDED // 3, out_dtype=BF16)
            proj = proj.reshape(b, s, EV_COLS_PADDED)
            a_out = _dsa(proj, nbatch=2 if b % 2 == 0 else 1)
            g_out = _gla(proj, ev_g_w2[j].astype(BF16), row(ev_g_b2[j]), ev_g_norm[j])
            w_out = ev_w_out[j].astype(BF16)
            acts = [a_out.reshape(t, a_width), g_out.reshape(t, -1)]
            weights = [w_out[:a_width], w_out[a_width:]]
        else:
            wg = jnp.pad(od_w_gate[j], ((0, 0), (0, LANES - 2 * M_HEADS))).astype(BF16)
            bg = jnp.pad(od_b_gate[j], (0, LANES - 2 * M_HEADS)).reshape(1, LANES)
            q, k, v, xc, op, gates = _mlstm_pre(
                xb.reshape(b, s, d), od_w_in[j].astype(BF16), od_conv_w[j], row(od_conv_b[j]),
                _block_diag(od_w_q[j]), _block_diag(od_w_k[j]), _block_diag(od_w_v[j]), wg, bg,
                tm=min(256, s))
            y = _mlstm(q, k, v, xc, op, gates, row(od_skip[j]), od_norm[j], chunk=min(MLSTM_CHUNK, s))
            acts = [y.reshape(t, M_INNER)]
            weights = [od_w_out[j].astype(BF16)]
        xf, xb = _proj_ln(acts, weights, xf, row(ln1_g[l]), row(ln1_b[l]), tm=tm)
        xf, xb = _mlp(xb, xf, mlp_w1[l].astype(BF16), mlp_w2[l].astype(BF16), row(ln2_g[l]), row(ln2_b[l]), tm=tm)
    return xf.reshape(b, s, d)
```

```python
import functools

import numpy as np
import jax
import jax.numpy as jnp
from jax import lax
from jax.experimental import pallas as pl
from jax.experimental.pallas import tpu as pltpu

F32 = jnp.float32
BF16 = jnp.bfloat16
I32 = jnp.int32

D_MODEL = 1024
DEPTH = 4
ALPHA = (2 * DEPTH) ** 0.25
LN_EPS = 1e-5
A_HEADS = 8
A_HEAD_DIM = 64
IDX_HEADS = 8
IDX_DIM = 64
TOPK_MAX = 256
Q_BLOCK = 128
KEY_CHUNK = 256
G_HEADS = 4
G_DK = 128
G_DV = 128
G_RANK = 16
G_TAU = 16.0
GLA_CHUNK = 128
M_INNER = 2 * D_MODEL
M_HEADS = 4
M_DH = M_INNER // M_HEADS
M_CONV = 4
M_QKV_BLOCK = 4
MLSTM_CHUNK = 256
D_FF = 4 * D_MODEL

LANES = 128
SUBLANES = 8
MXU_DIM = 256
VMEM_LIMIT = 56 * 1024 * 1024

EV_AQ, EV_IQ, EV_GQ, EV_GK, EV_GV, EV_GR = 0, 1, 2, 3, 4, 5
EV_KV, EV_KIW, EV_GLR = 24, 25, 26
EV_COLS_PADDED = 27 * LANES

INT_MIN = -2147483648
KEY_NEG_INF = INT_MIN - (-8388608)
MASK_BIAS = -2e30
M_INIT = -1e30
LOG2_E = 1.4426950408889634


def _cparams(sem):
    return pltpu.CompilerParams(dimension_semantics=sem, vmem_limit_bytes=VMEM_LIMIT)


def _layer_norm(z, g, b):
    mu = jnp.mean(z, axis=-1, keepdims=True)
    zc = z - mu
    var = jnp.mean(zc * zc, axis=-1, keepdims=True)
    return zc * lax.rsqrt(var + LN_EPS) * g + b


def _dot(a, b):
    return jnp.dot(a, b, preferred_element_type=F32)


def _dot_nt(a, b):
    return lax.dot_general(a, b, (((1,), (1,)), ((), ())), preferred_element_type=F32)


def _dot_tn(a, b):
    return lax.dot_general(a, b, (((0,), (0,)), ((), ())), preferred_element_type=F32)


def _split_bf16(x):
    hi = x.astype(BF16)
    lo = (x - hi.astype(F32)).astype(BF16)
    return jnp.concatenate([hi, lo], axis=-1)


def _log_sigmoid(x):
    return jnp.minimum(x, 0.0) - jnp.log1p(jnp.exp(-jnp.abs(x)))


def _matmul_kernel(x_ref, w_ref, o_ref):
    o_ref[...] = _dot(x_ref[...], w_ref[...]).astype(o_ref.dtype)


def _matmul(x, w, *, tm, tn, out_dtype):
    t, k = x.shape
    n = w.shape[1]
    return pl.pallas_call(
        _matmul_kernel,
        grid=(t // tm, n // tn),
        in_specs=[pl.BlockSpec((tm, k), lambda i, j: (i, 0)),
                  pl.BlockSpec((k, tn), lambda i, j: (0, j))],
        out_specs=pl.BlockSpec((tm, tn), lambda i, j: (i, j)),
        out_shape=jax.ShapeDtypeStruct((t, n), out_dtype),
        compiler_params=_cparams(("parallel", "arbitrary")),
        name="in_proj",
    )(x, w)


def _proj_ln_kernel(*refs, n_in):
    a_refs, w_refs = refs[:n_in], refs[n_in:2 * n_in]
    xf_ref, g_ref, b_ref, of_ref, ob_ref = refs[2 * n_in:]
    acc = _dot(a_refs[0][...], w_refs[0][...])
    for a_ref, w_ref in zip(a_refs[1:], w_refs[1:]):
        acc = acc + _dot(a_ref[...], w_ref[...])
    y = _layer_norm(ALPHA * xf_ref[...] + acc, g_ref[...], b_ref[...])
    of_ref[...] = y
    ob_ref[...] = y.astype(BF16)


def _proj_ln(acts, weights, xf, g, b, *, tm):
    t, d = xf.shape
    in_specs = [pl.BlockSpec((tm, a.shape[1]), lambda i: (i, 0)) for a in acts]
    in_specs += [pl.BlockSpec(w.shape, lambda i: (0, 0)) for w in weights]
    in_specs += [pl.BlockSpec((tm, d), lambda i: (i, 0)),
                 pl.BlockSpec((1, d), lambda i: (0, 0)),
                 pl.BlockSpec((1, d), lambda i: (0, 0))]
    return pl.pallas_call(
        functools.partial(_proj_ln_kernel, n_in=len(acts)),
        grid=(t // tm,),
        in_specs=in_specs,
        out_specs=[pl.BlockSpec((tm, d), lambda i: (i, 0)),
                   pl.BlockSpec((tm, d), lambda i: (i, 0))],
        out_shape=[jax.ShapeDtypeStruct((t, d), F32), jax.ShapeDtypeStruct((t, d), BF16)],
        compiler_params=_cparams(("parallel",)),
        name="out_proj_ln",
    )(*acts, *weights, xf, g, b)


def _mlp_kernel(xb_ref, xf_ref, w1_ref, w2_ref, g_ref, b_ref, of_ref, ob_ref, *, ff_chunk):
    xb = xb_ref[...]
    acc = None
    for c in range(D_FF // ff_chunk):
        h = _dot(xb, w1_ref[:, c * ff_chunk:(c + 1) * ff_chunk])
        h = jnp.maximum(h, 0.0)
        part = _dot((h * h).astype(BF16), w2_ref[c * ff_chunk:(c + 1) * ff_chunk, :])
        acc = part if acc is None else acc + part
    y = _layer_norm(ALPHA * xf_ref[...] + acc, g_ref[...], b_ref[...])
    of_ref[...] = y
    ob_ref[...] = y.astype(BF16)


def _mlp(xb, xf, w1, w2, g, b, *, tm, ff_chunk=1024):
    t, d = xf.shape
    const = lambda i: (0, 0)
    return pl.pallas_call(
        functools.partial(_mlp_kernel, ff_chunk=ff_chunk),
        grid=(t // tm,),
        in_specs=[pl.BlockSpec((tm, d), lambda i: (i, 0)),
                  pl.BlockSpec((tm, d), lambda i: (i, 0)),
                  pl.BlockSpec(w1.shape, const, pipeline_mode=pl.Buffered(1)),
                  pl.BlockSpec(w2.shape, const, pipeline_mode=pl.Buffered(1)),
                  pl.BlockSpec((1, d), const),
                  pl.BlockSpec((1, d), const)],
        out_specs=[pl.BlockSpec((tm, d), lambda i: (i, 0)),
                   pl.BlockSpec((tm, d), lambda i: (i, 0))],
        out_shape=[jax.ShapeDtypeStruct((t, d), F32), jax.ShapeDtypeStruct((t, d), BF16)],
        compiler_params=_cparams(("parallel",)),
        name="mlp_ln",
    )(xb, xf, w1, w2, g, b)


def _sortable_key(x):
    bits = pltpu.bitcast(x, I32)
    return jnp.where(bits < 0, INT_MIN - bits, bits)


def _column_count(hit):
    quarter = jnp.sum(hit.reshape(4, KEY_CHUNK // 4, LANES), axis=0)
    return jnp.sum(quarter.reshape(KEY_CHUNK // 4 // SUBLANES, SUBLANES, LANES), axis=0)


def _dsa_kernel(q_ref, qi_ref, kv_ref, kiw_ref, o_ref,
                qi_s, qa_s, skey_s, skey16_s, thr_s, bias_s, s_s, cmax_s, p_s, m_s, l_s, acc_s, *, nbatch, topk):
    i = pl.program_id(1)
    nch = i // 2 + 1
    key_in_chunk = lax.broadcasted_iota(I32, (KEY_CHUNK, Q_BLOCK), 0)
    q_in_block = lax.broadcasted_iota(I32, (KEY_CHUNK, Q_BLOCK), 1)
    nch_max = skey_s.shape[1]

    sel_r = lax.broadcasted_iota(I32, (2 * SUBLANES, LANES), 0)
    sel_c = lax.broadcasted_iota(I32, (2 * SUBLANES, LANES), 1)
    sel = jnp.where(sel_c == sel_r + IDX_DIM, 1.0, 0.0).astype(BF16)
    w_rows = []
    for g in range(nbatch):
        qi = qi_ref[g]
        q = q_ref[g]
        for h in range(IDX_HEADS):
            qi_s[g, h * Q_BLOCK:(h + 1) * Q_BLOCK, :] = qi[:, h * IDX_DIM:(h + 1) * IDX_DIM]
        for h in range(A_HEADS):
            qh = q[:, h * A_HEAD_DIM:(h + 1) * A_HEAD_DIM].astype(F32) * (A_HEAD_DIM ** -0.5 * LOG2_E)
            qa_s[g, h * Q_BLOCK:(h + 1) * Q_BLOCK, :] = qh.astype(BF16)
        w_blk = kiw_ref[g, pl.ds(pl.multiple_of(i * Q_BLOCK, Q_BLOCK), Q_BLOCK), :]
        w_rows.append(_dot_nt(sel, w_blk)[:IDX_HEADS] * (IDX_HEADS ** -0.5 * IDX_DIM ** -0.5))
        m_s[g] = jnp.full(m_s.shape[1:], M_INIT, F32)
        l_s[g] = jnp.zeros(l_s.shape[1:], F32)
        acc_s[g] = jnp.zeros(acc_s.shape[1:], F32)

    npair = (nch + 1) // 2

    def chunk_start(c):
        return pl.multiple_of(jnp.minimum(c, nch - 1) * KEY_CHUNK, KEY_CHUNK)

    def store_heads(g, slot, res):
        for h in range(A_HEADS):
            s_s[g, slot, h] = res[:, h * Q_BLOCK:(h + 1) * Q_BLOCK]

    def logits_into(c, slot):
        for g in range(nbatch):
            ki = kiw_ref[g, pl.ds(chunk_start(c), KEY_CHUNK), :][:, :IDX_DIM]
            store_heads(g, slot, _dot_nt(ki, qi_s[g]))

    def score_from(c, slot):
        valid = (chunk_start(c) + key_in_chunk) <= (i * Q_BLOCK + q_in_block)
        for g in range(nbatch):
            sc = None
            for h in range(IDX_HEADS):
                t = w_rows[g][h:h + 1, :] * jnp.maximum(s_s[g, slot, h], 0.0)
                sc = t if sc is None else sc + t
            key = _sortable_key(jnp.where(valid, sc, -jnp.inf))
            skey_s[g, jnp.minimum(c, nch - 1)] = key
            skey16_s[g, jnp.minimum(c, nch - 1)] = lax.shift_right_arithmetic(key, 16).astype(jnp.int16)

    def score_body(j, carry):
        logits_into(2 * j + 1, 1)
        score_from(2 * j, 0)
        logits_into(2 * j + 2, 0)
        score_from(2 * j + 1, 1)
        return carry

    logits_into(0, 0)
    lax.fori_loop(0, npair, score_body, 0)

    def count_where(pred):
        def body(c, accs):
            return tuple(accs[g] + _column_count(jnp.where(pred(g, c, skey_s[g, c]), 1, 0))
                         for g in range(nbatch))
        zero = tuple(jnp.zeros((SUBLANES, LANES), I32) for _ in range(nbatch))
        accs = lax.fori_loop(0, nch, body, zero)
        return [jnp.sum(a, axis=0, keepdims=True) for a in accs]

    for g in range(nbatch):
        thr_s[g] = jnp.full(thr_s.shape[1:], KEY_NEG_INF + 1, I32)

    def count_high_ge(cands):
        rows = 2 * SUBLANES
        cands16 = [jnp.broadcast_to(cd, (rows, LANES)).astype(jnp.int16) for cd in cands]

        def body(c, accs):
            out = []
            for g in range(nbatch):
                hi = skey16_s[g, c]
                parts = [jnp.where(hi[r * rows:(r + 1) * rows] >= cands16[g], jnp.bfloat16(1), jnp.bfloat16(0))
                         for r in range(KEY_CHUNK // rows)]
                while len(parts) > 1:
                    parts = [parts[a] + parts[a + 1] for a in range(0, len(parts), 2)]
                out.append(accs[g] + parts[0])
            return tuple(out)

        zero = tuple(jnp.zeros((rows, LANES), BF16) for _ in range(nbatch))
        accs = lax.fori_loop(0, nch, body, zero)
        return [jnp.sum(a.astype(F32), axis=0, keepdims=True) for a in accs]

    @pl.when(i * Q_BLOCK >= topk)
    def _search():
        def high_body(it, los):
            inc = lax.shift_left(jnp.int32(1), 15 - it)
            cands = [lo + inc for lo in los]
            cnt = count_high_ge(cands)
            return tuple(jnp.where(cnt[g] >= topk, cands[g], los[g]) for g in range(nbatch))

        highs = lax.fori_loop(0, 16, high_body,
                              tuple(jnp.full((1, LANES), -32768, I32) for _ in range(nbatch)))

        above = count_high_ge([hi + 1 for hi in highs])
        need_low = [topk - above[g] for g in range(nbatch)]
        rows = 2 * SUBLANES
        h16 = [jnp.broadcast_to(hi, (rows, LANES)).astype(jnp.int16) for hi in highs]

        def bucket_body(c, carry):
            for g in range(nbatch):
                low = (lax.bitwise_and(skey_s[g, c], 0xFFFF) - 32768).astype(jnp.int16)
                for r in range(KEY_CHUNK // rows):
                    rs = slice(r * rows, (r + 1) * rows)
                    skey16_s[g, c, rs, :] = jnp.where(skey16_s[g, c, rs, :] == h16[g], low[rs], jnp.int16(-32768))
            return carry

        lax.fori_loop(0, nch, bucket_body, 0)

        def low_body(it, los):
            inc = lax.shift_left(jnp.int32(1), 15 - it)
            cands = [lo + inc for lo in los]
            cnt = count_high_ge(cands)
            return tuple(jnp.where(cnt[g] >= need_low[g], cands[g], los[g]) for g in range(nbatch))

        lows = lax.fori_loop(0, 16, low_body,
                             tuple(jnp.full((1, LANES), -32768, I32) for _ in range(nbatch)))
        los = [lax.shift_left(highs[g], 16) + (lows[g] + 32768) for g in range(nbatch)]
        for g in range(nbatch):
            thr_s[g] = jnp.broadcast_to(los[g], thr_s.shape[1:])

        cnt_ge = count_where(lambda g, c, kk: kk >= los[g])
        worst = cnt_ge[0]
        for g in range(1, nbatch):
            worst = jnp.maximum(worst, cnt_ge[g])

        @pl.when(jnp.max(worst) > topk)
        def _ties():
            cnt_gt = count_where(lambda g, c, kk: kk > los[g])
            need = [topk - cnt_gt[g] for g in range(nbatch)]
            nbits = max(1, int(np.ceil(np.log2(nch_max * KEY_CHUNK))))

            def idx_body(it, ps):
                inc = lax.shift_left(jnp.int32(1), nbits - 1 - it)

                def pred(g, c, kk):
                    idx = c * KEY_CHUNK + key_in_chunk
                    return jnp.logical_and(kk == los[g], idx < ps[g] + inc)

                cnt = count_where(pred)
                return tuple(jnp.where(cnt[g] < need[g], ps[g] + inc, ps[g]) for g in range(nbatch))

            ps = lax.fori_loop(0, nbits, idx_body,
                               tuple(jnp.zeros((1, LANES), I32) for _ in range(nbatch)))

            def demote_body(c, carry):
                idx = c * KEY_CHUNK + key_in_chunk
                for g in range(nbatch):
                    kk = skey_s[g, c]
                    drop = jnp.logical_and(kk == los[g], idx > ps[g])
                    skey_s[g, c] = jnp.where(drop, INT_MIN, kk)
                return carry

            lax.fori_loop(0, nch, demote_body, 0)

    def qk_into(c, slot):
        cc = jnp.minimum(c, nch - 1)
        for g in range(nbatch):
            k = kv_ref[g, pl.ds(chunk_start(c), KEY_CHUNK), :][:, :A_HEAD_DIM]
            thr = jnp.where(c < nch, thr_s[g][0:1, :], jnp.int32(2147483647))
            bias_s[g, slot] = jnp.where(skey_s[g, cc] >= thr, 0.0, MASK_BIAS)
            res = _dot_nt(k, qa_s[g])
            cmax = []
            for h in range(A_HEADS):
                sh = res[:, h * Q_BLOCK:(h + 1) * Q_BLOCK] + bias_s[g, slot]
                s_s[g, slot, h] = sh
                cmax.append(jnp.max(sh, axis=0, keepdims=True))
            cmax_s[g, slot] = jnp.broadcast_to(jnp.concatenate(cmax, axis=1), cmax_s.shape[2:])

    def softmax_from(c, slot):
        for g in range(nbatch):
            v = kv_ref[g, pl.ds(chunk_start(c), KEY_CHUNK), :][:, A_HEAD_DIM:]
            m_prev = m_s[g][0:1, :]
            m_new = jnp.maximum(m_prev, cmax_s[g, slot][0:1, :])
            alpha = jnp.exp2(m_prev - m_new)
            sums = []
            for h in range(A_HEADS):
                cs = slice(h * Q_BLOCK, (h + 1) * Q_BLOCK)
                p = jnp.exp2(s_s[g, slot, h] - m_new[:, cs])
                p_s[g, slot, :, cs] = p.astype(BF16)
                sums.append(jnp.sum(p, axis=0, keepdims=True))
            l_s[g] = jnp.broadcast_to(alpha * l_s[g][0:1, :] + jnp.concatenate(sums, axis=1), l_s.shape[1:])
            m_s[g] = jnp.broadcast_to(m_new, m_s.shape[1:])
            acc_s[g] = acc_s[g] * alpha + _dot_tn(v, p_s[g, slot])

    def attn_body(j, carry):
        qk_into(2 * j + 1, 1)
        softmax_from(2 * j, 0)
        qk_into(2 * j + 2, 0)
        softmax_from(2 * j + 1, 1)
        return carry

    qk_into(0, 0)
    lax.fori_loop(0, npair, attn_body, 0)

    for g in range(nbatch):
        out_t = acc_s[g] / l_s[g][0:1, :]
        outs = [out_t[:, h * Q_BLOCK:(h + 1) * Q_BLOCK].T for h in range(A_HEADS)]
        o_ref[g] = jnp.concatenate(outs, axis=1).astype(o_ref.dtype)


def _dsa(proj, *, nbatch):
    b, s, _ = proj.shape
    topk = min(TOPK_MAX, s // 4)
    assert topk % Q_BLOCK == 0 and s % KEY_CHUNK == 0 and b % nbatch == 0
    nb = s // Q_BLOCK
    nch_max = s // KEY_CHUNK
    width = A_HEADS * A_HEAD_DIM
    return pl.pallas_call(
        functools.partial(_dsa_kernel, nbatch=nbatch, topk=topk),
        grid=(b // nbatch, nb),
        in_specs=[pl.BlockSpec((nbatch, Q_BLOCK, width), lambda bi, i: (bi, i, EV_AQ)),
                  pl.BlockSpec((nbatch, Q_BLOCK, width), lambda bi, i: (bi, i, EV_IQ)),
                  pl.BlockSpec((nbatch, s, LANES), lambda bi, i: (bi, 0, EV_KV), pipeline_mode=pl.Buffered(1)),
                  pl.BlockSpec((nbatch, s, LANES), lambda bi, i: (bi, 0, EV_KIW), pipeline_mode=pl.Buffered(1))],
        out_specs=pl.BlockSpec((nbatch, Q_BLOCK, width), lambda bi, i: (bi, i, 0)),
        out_shape=jax.ShapeDtypeStruct((b, s, width), BF16),
        scratch_shapes=[
            pltpu.VMEM((nbatch, IDX_HEADS * Q_BLOCK, IDX_DIM), BF16),
            pltpu.VMEM((nbatch, A_HEADS * Q_BLOCK, A_HEAD_DIM), BF16),
            pltpu.VMEM((nbatch, nch_max, KEY_CHUNK, Q_BLOCK), I32),
            pltpu.VMEM((nbatch, nch_max, KEY_CHUNK, Q_BLOCK), jnp.int16),
            pltpu.VMEM((nbatch, SUBLANES, Q_BLOCK), I32),
            pltpu.VMEM((nbatch, 2, KEY_CHUNK, Q_BLOCK), F32),
            pltpu.VMEM((nbatch, 2, A_HEADS, KEY_CHUNK, Q_BLOCK), F32),
            pltpu.VMEM((nbatch, 2, SUBLANES, A_HEADS * Q_BLOCK), F32),
            pltpu.VMEM((nbatch, 2, KEY_CHUNK, A_HEADS * Q_BLOCK), BF16),
            pltpu.VMEM((nbatch, SUBLANES, A_HEADS * Q_BLOCK), F32),
            pltpu.VMEM((nbatch, SUBLANES, A_HEADS * Q_BLOCK), F32),
            pltpu.VMEM((nbatch, A_HEAD_DIM, A_HEADS * Q_BLOCK), F32),
        ],
        compiler_params=_cparams(("parallel", "arbitrary")),
        name="dsa",
    )(proj, proj, proj, proj)


def _gla_constants(chunk):
    t = np.arange(chunk)[:, None]
    r = np.arange(chunk)[None, :]
    blocks = [(r <= t), (r > t)]
    masks = [(t == r)]
    n = chunk // 2
    while n >= 1:
        upper = (t // n) % 2 == 1
        mid = (t // n) * n
        end = (t // n + 1) * n - 1
        blocks.append((upper & (r >= mid) & (r <= t)) | (~upper & (r >= t + 1) & (r <= end)))
        s = r
        masks.append(upper & ((s // n) % 2 == 0) & (t // (2 * n) == s // (2 * n)))
        n //= 2
    return (np.concatenate(blocks, axis=0).astype(np.float32),
            np.stack(masks, axis=0).astype(np.float32))


def _gla_kernel(q_ref, k_ref, v_ref, r_ref, lr_ref, w2_ref, b2_ref, gn_ref, sums_ref, masks_ref,
                o_ref, state_s, *, chunk):
    nlev = masks_ref.shape[0] - 1

    @pl.when(pl.program_id(1) == 0)
    def _init():
        state_s[...] = jnp.zeros_like(state_s)

    lr = lr_ref[0][:, :G_RANK]
    sums = sums_ref[...]
    for h in range(G_HEADS):
        cs = slice(h * G_DK, (h + 1) * G_DK)
        x = _dot(lr, w2_ref[:, cs]) + b2_ref[:, cs]
        la = _log_sigmoid(x) * (1.0 / G_TAU)
        dec = _dot(sums, _split_bf16(la))
        e = jnp.exp(dec[:, :G_DK] + dec[:, G_DK:])
        qf = q_ref[0][:, cs].astype(F32) * G_DK ** -0.5
        kb = k_ref[0][:, cs]
        kf = kb.astype(F32)
        vb = v_ref[0][:, cs]
        st = state_s[h]

        o = _dot_nt((qf * e[0:chunk]).astype(BF16), st.astype(BF16))
        att = masks_ref[0] * _dot_nt(qf.astype(BF16), kb)
        for lv in range(nlev):
            e_lv = e[(2 + lv) * chunk:(3 + lv) * chunk]
            att = att + masks_ref[lv + 1] * _dot_nt((qf * e_lv).astype(BF16), (kf * e_lv).astype(BF16))
        o = o + _dot(att.astype(BF16), vb)

        kd = (kf * e[chunk:2 * chunk]).astype(BF16)
        state_s[h] = st * e[chunk - 1:chunk] + _dot_tn(vb, kd)

        mu = jnp.mean(o, axis=-1, keepdims=True)
        oc = o - mu
        var = jnp.mean(oc * oc, axis=-1, keepdims=True)
        on = oc * lax.rsqrt(var + LN_EPS) * gn_ref[h:h + 1, :]
        rg = r_ref[0][:, cs].astype(F32)
        o_ref[0, :, cs] = (on * (rg * jax.nn.sigmoid(rg))).astype(o_ref.dtype)


def _gla(proj, w2, b2, gnorm, *, chunk=GLA_CHUNK):
    b, s, _ = proj.shape
    width = G_HEADS * G_DK
    sums, masks = _gla_constants(chunk)
    sums = jnp.asarray(sums, BF16)
    masks = jnp.asarray(masks, F32)
    col = lambda blk: pl.BlockSpec((1, chunk, width), lambda bi, c: (bi, c, blk))
    const2 = lambda bi, c: (0, 0)
    return pl.pallas_call(
        functools.partial(_gla_kernel, chunk=chunk),
        grid=(b, s // chunk),
        in_specs=[col(EV_GQ), col(EV_GK), col(EV_GV), col(EV_GR),
                  pl.BlockSpec((1, chunk, LANES), lambda bi, c: (bi, c, EV_GLR)),
                  pl.BlockSpec(w2.shape, const2),
                  pl.BlockSpec(b2.shape, const2),
                  pl.BlockSpec(gnorm.shape, const2),
                  pl.BlockSpec(sums.shape, const2),
                  pl.BlockSpec(masks.shape, lambda bi, c: (0, 0, 0))],
        out_specs=pl.BlockSpec((1, chunk, width), lambda bi, c: (bi, c, 0)),
        out_shape=jax.ShapeDtypeStruct((b, s, width), BF16),
        scratch_shapes=[pltpu.VMEM((G_HEADS, G_DV, G_DK), F32)],
        compiler_params=_cparams(("parallel", "arbitrary")),
        name="gla",
    )(proj, proj, proj, proj, proj, w2, b2, gnorm, sums, masks)


def _mlstm_pre_kernel(x_ref, win_ref, cw_ref, cb_ref, wq_ref, wk_ref, wv_ref, wg_ref, bg_ref,
                      q_ref, k_ref, v_ref, xc_ref, op_ref, gate_ref, ext_s, *, tm):
    @pl.when(pl.program_id(1) == 0)
    def _init():
        ext_s[0:SUBLANES, :] = jnp.zeros((SUBLANES, M_INNER), F32)

    xb = x_ref[0]
    ext_s[SUBLANES:, :] = _dot(xb, win_ref[:, :M_INNER])
    op_ref[0] = _dot(xb, win_ref[:, M_INNER:]).astype(op_ref.dtype)
    gates = bg_ref[...]
    for gi in range(M_INNER // MXU_DIM):
        cs = slice(gi * MXU_DIM, (gi + 1) * MXU_DIM)
        conv = cb_ref[:, cs]
        for j in range(M_CONV):
            off = SUBLANES - (M_CONV - 1) + j
            conv = conv + cw_ref[j:j + 1, cs] * ext_s[off:off + tm, cs]
        x_c = conv * jax.nn.sigmoid(conv)
        xcb = x_c.astype(BF16)
        xc_ref[0, :, cs] = xcb
        xmb = ext_s[SUBLANES:, cs].astype(BF16)
        for src, w_ref, dst, part in ((xcb, wq_ref, q_ref, 0), (xcb, wk_ref, k_ref, 1), (xmb, wv_ref, v_ref, 2)):
            y = _dot(src, w_ref[gi]).astype(BF16)
            dst[0, :, cs] = y
            gates = gates + _dot(y, wg_ref[part * M_INNER + gi * MXU_DIM:part * M_INNER + (gi + 1) * MXU_DIM, :])
    ext_s[0:SUBLANES, :] = ext_s[tm:tm + SUBLANES, :]
    gate_ref[0] = gates[:, :2 * M_HEADS]


def _mlstm_pre(xb, win, cw, cb, wq, wk, wv, wg, bg, *, tm):
    b, s, d = xb.shape
    const2 = lambda bi, t: (0, 0)
    const3 = lambda bi, t: (0, 0, 0)
    tile = lambda dt: jax.ShapeDtypeStruct((b, s, M_INNER), dt)
    ospec = pl.BlockSpec((1, tm, M_INNER), lambda bi, t: (bi, t, 0))
    return pl.pallas_call(
        functools.partial(_mlstm_pre_kernel, tm=tm),
        grid=(b, s // tm),
        in_specs=[pl.BlockSpec((1, tm, d), lambda bi, t: (bi, t, 0)),
                  pl.BlockSpec(win.shape, const2, pipeline_mode=pl.Buffered(1)),
                  pl.BlockSpec(cw.shape, const2),
                  pl.BlockSpec(cb.shape, const2),
                  pl.BlockSpec(wq.shape, const3),
                  pl.BlockSpec(wk.shape, const3),
                  pl.BlockSpec(wv.shape, const3),
                  pl.BlockSpec(wg.shape, const2),
                  pl.BlockSpec(bg.shape, const2)],
        out_specs=[ospec, ospec, ospec, ospec, ospec,
                   pl.BlockSpec((1, tm, 2 * M_HEADS), lambda bi, t: (bi, t, 0))],
        out_shape=[tile(BF16), tile(BF16), tile(BF16), tile(BF16), tile(BF16),
                   jax.ShapeDtypeStruct((b, s, 2 * M_HEADS), F32)],
        scratch_shapes=[pltpu.VMEM((tm + SUBLANES, M_INNER), F32)],
        compiler_params=_cparams(("parallel", "arbitrary")),
        name="mlstm_pre",
    )(xb, win, cw, cb, wq, wk, wv, wg, bg)


def _mlstm_kernel(q_ref, k_ref, v_ref, xc_ref, op_ref, gc_ref, gr_ref, skip_ref, ng_ref, tri_ref, ones_ref,
                  y_ref, c_s, n_s, m_s, num_s, rden_s, *, chunk):
    @pl.when(pl.program_id(1) == 0)
    def _init():
        c_s[...] = jnp.zeros_like(c_s)
        n_s[...] = jnp.zeros_like(n_s)
        m_s[...] = jnp.zeros_like(m_s)

    tri = tri_ref[...]
    t_idx = lax.broadcasted_iota(I32, (chunk, chunk), 0)
    s_idx = lax.broadcasted_iota(I32, (chunk, chunk), 1)
    causal = s_idx <= t_idx
    gc = gc_ref[0]
    gr = gr_ref[0]
    for h in range(M_HEADS):
        cs = slice(h * M_DH, (h + 1) * M_DH)
        li_c = gc[:, h:h + 1]
        lf_c = _log_sigmoid(gc[:, M_HEADS + h:M_HEADS + h + 1])
        li_r = gr[h:h + 1, :]
        lf_r = _log_sigmoid(gr[M_HEADS + h:M_HEADS + h + 1, :])
        bc2 = _dot(tri, _split_bf16(jnp.broadcast_to(lf_c, (chunk, LANES))))
        b_c = (bc2[:, :LANES] + bc2[:, LANES:])[:, 0:1]
        lf_r8 = jnp.broadcast_to(lf_r, (SUBLANES, chunk))
        hi = lf_r8.astype(BF16)
        lo = (lf_r8 - hi.astype(F32)).astype(BF16)
        br2 = _dot_nt(jnp.concatenate([hi, lo], axis=0), tri)
        b_r = br2[0:1, :] + br2[SUBLANES:SUBLANES + 1, :]

        m_prev = m_s[h][0:1, 0:1]
        dmat = jnp.where(causal, b_c - b_r + li_r, -jnp.inf)
        inter = b_c + m_prev
        m_t = jnp.maximum(inter, jnp.max(dmat, axis=1, keepdims=True))
        w_inter = jnp.exp(inter - m_t)

        qb = q_ref[0, :, cs]
        kf = k_ref[0, :, cs].astype(F32) * M_DH ** -0.5
        vb = v_ref[0, :, cs]
        n_prev = n_s[h][0:1, :]

        pb = (jnp.exp(dmat - m_t) * _dot_nt(qb, kf.astype(BF16))).astype(BF16)
        num_s[...] = w_inter * _dot(qb, c_s[h].astype(BF16)) + _dot(pb, vb)
        qn = _dot_nt(qb, n_s[h].astype(BF16))[:, 0:1]
        den = w_inter * qn + _dot(pb, ones_ref[...])[:, 0:1]
        rden_s[...] = jnp.broadcast_to(1.0 / jnp.maximum(jnp.abs(den), jnp.exp(-m_t)), rden_s.shape)

        g_tot = b_c[chunk - 1:chunk, :]
        d_end = g_tot - b_c + li_c
        m_new = jnp.maximum(g_tot + m_prev, jnp.max(d_end, axis=0, keepdims=True))
        wk = jnp.exp(d_end - m_new)
        decay = jnp.exp(g_tot + m_prev - m_new)
        kw = kf * wk
        kwb = kw.astype(BF16)
        n_s[h] = jnp.broadcast_to(decay * n_prev + jnp.sum(kw, axis=0, keepdims=True), n_s.shape[1:])
        m_s[h] = jnp.broadcast_to(m_new, m_s.shape[1:])
        for r in range(M_DH // LANES):
            rs = slice(r * LANES, (r + 1) * LANES)
            c_s[h, rs, :] = decay * c_s[h, rs, :] + _dot_tn(kwb[:, rs], vb)

        slab = min(chunk, 64)
        for r in range(chunk // slab):
            rs = slice(r * slab, (r + 1) * slab)
            hg = num_s[rs, :] * rden_s[rs, 0:1] * jax.nn.sigmoid(op_ref[0, rs, cs].astype(F32))
            mu = jnp.mean(hg, axis=-1, keepdims=True)
            hc = hg - mu
            var = jnp.mean(hc * hc, axis=-1, keepdims=True)
            hn = hc * lax.rsqrt(var + LN_EPS) * ng_ref[h:h + 1, :]
            y_ref[0, rs, cs] = (hn + skip_ref[:, cs] * xc_ref[0, rs, cs].astype(F32)).astype(y_ref.dtype)


def _mlstm(q, k, v, xc, op, gates, skip, norm_g, *, chunk=MLSTM_CHUNK):
    b, s, _ = q.shape
    gates_t = jnp.swapaxes(gates, 1, 2)
    tri = jnp.asarray(np.tril(np.ones((chunk, chunk), np.float32)), BF16)
    ones = jnp.ones((chunk, LANES), BF16)
    big = pl.BlockSpec((1, chunk, M_INNER), lambda bi, c: (bi, c, 0))
    const2 = lambda bi, c: (0, 0)
    return pl.pallas_call(
        functools.partial(_mlstm_kernel, chunk=chunk),
        grid=(b, s // chunk),
        in_specs=[big, big, big, big, big,
                  pl.BlockSpec((1, chunk, 2 * M_HEADS), lambda bi, c: (bi, c, 0)),
                  pl.BlockSpec((1, 2 * M_HEADS, chunk), lambda bi, c: (bi, 0, c)),
                  pl.BlockSpec(skip.shape, const2),
                  pl.BlockSpec(norm_g.shape, const2),
                  pl.BlockSpec(tri.shape, const2),
                  pl.BlockSpec(ones.shape, const2)],
        out_specs=big,
        out_shape=jax.ShapeDtypeStruct((b, s, M_INNER), BF16),
        scratch_shapes=[pltpu.VMEM((M_HEADS, M_DH, M_DH), F32),
                        pltpu.VMEM((M_HEADS, SUBLANES, M_DH), F32),
                        pltpu.VMEM((M_HEADS, SUBLANES, LANES), F32),
                        pltpu.VMEM((chunk, M_DH), F32),
                        pltpu.VMEM((chunk, LANES), F32)],
        compiler_params=_cparams(("parallel", "arbitrary")),
        name="mlstm",
    )(q, k, v, xc, op, gates, gates_t, skip, norm_g, tri, ones)


def _even_w_in(w):
    sizes = (512, 64, 64, 512, 64, 8, 512, 512, 512, 512, 16)
    offs = np.concatenate([[0], np.cumsum(sizes)])
    part = lambda j: w[:, offs[j]:offs[j + 1]]
    a_q, a_k, a_v, i_q, i_k, i_w, g_q, g_k, g_v, g_r, g_lr = (part(j) for j in range(11))
    zeros = lambda n: jnp.zeros((w.shape[0], n), w.dtype)
    cols = [a_q, i_q, g_q, g_k, g_v, g_r, a_k, a_v, i_k, i_w, zeros(LANES - 72), g_lr, zeros(LANES - G_RANK)]
    return jnp.concatenate(cols, axis=1).astype(BF16)


def _block_diag(w):
    per = MXU_DIM // M_QKV_BLOCK
    wg = w.reshape(-1, per, M_QKV_BLOCK, M_QKV_BLOCK)
    eye = jnp.eye(per, dtype=w.dtype)
    dense = jnp.einsum('gade,ab->gadbe', wg, eye)
    return dense.reshape(-1, MXU_DIM, MXU_DIM).astype(BF16)


def _token_tile(t):
    for tm in (512, 256, 128):
        if t % tm == 0:
            return tm
    raise ValueError(f"token count {t} is not a multiple of 128")


def kernel(x, ev_w_in, ev_g_w2, ev_g_b2, ev_g_norm, ev_w_out, od_w_in, od_conv_w, od_conv_b, od_w_q, od_w_k, od_w_v, od_w_gate, od_b_gate, od_skip, od_norm, od_w_out, ln1_g, ln1_b, ln2_g, ln2_b, mlp_w1, mlp_w2):
    b, s, d = x.shape
    t = b * s
    tm = _token_tile(t)
    xf = x.reshape(t, d)
    xb = xf.astype(BF16)
    row = lambda v: v.reshape(1, -1)
    a_width = A_HEADS * A_HEAD_DIM
    for l in range(DEPTH):
        j = l // 2
        if l % 2 == 0:
            proj = _matmul(xb, _even_w_in(ev_w_in[j]), tm=tm, tn=EV_COLS_PADDED // 3, out_dtype=BF16)
            proj = proj.reshape(b, s, EV_COLS_PADDED)
            a_out = _dsa(proj, nbatch=next(n for n in (4, 2, 1) if b % n == 0))
            g_out = _gla(proj, ev_g_w2[j].astype(BF16), row(ev_g_b2[j]), ev_g_norm[j])
            w_out = ev_w_out[j].astype(BF16)
            acts = [a_out.reshape(t, a_width), g_out.reshape(t, -1)]
            weights = [w_out[:a_width], w_out[a_width:]]
        else:
            wg = jnp.pad(od_w_gate[j], ((0, 0), (0, LANES - 2 * M_HEADS))).astype(BF16)
            bg = jnp.pad(od_b_gate[j], (0, LANES - 2 * M_HEADS)).reshape(1, LANES)
            q, k, v, xc, op, gates = _mlstm_pre(
                xb.reshape(b, s, d), od_w_in[j].astype(BF16), od_conv_w[j], row(od_conv_b[j]),
                _block_diag(od_w_q[j]), _block_diag(od_w_k[j]), _block_diag(od_w_v[j]), wg, bg,
                tm=min(256, s))
            y = _mlstm(q, k, v, xc, op, gates, row(od_skip[j]), od_norm[j], chunk=min(MLSTM_CHUNK, s))
            acts = [y.reshape(t, M_INNER)]
            weights = [od_w_out[j].astype(BF16)]
        xf, xb = _proj_ln(acts, weights, xf, row(ln1_g[l]), row(ln1_b[l]), tm=tm)
        xf, xb = _mlp(xb, xf, mlp_w1[l].astype(BF16), mlp_w2[l].astype(BF16), row(ln2_g[l]), row(ln2_b[l]), tm=tm)
    return xf.reshape(b, s, d)
```

```python
import functools

import numpy as np
import jax
import jax.numpy as jnp
from jax import lax
from jax.experimental import pallas as pl
from jax.experimental.pallas import tpu as pltpu

F32 = jnp.float32
BF16 = jnp.bfloat16
I32 = jnp.int32

D_MODEL = 1024
DEPTH = 4
ALPHA = (2 * DEPTH) ** 0.25
LN_EPS = 1e-5
A_HEADS = 8
A_HEAD_DIM = 64
IDX_HEADS = 8
IDX_DIM = 64
TOPK_MAX = 256
Q_BLOCK = 128
KEY_CHUNK = 256
G_HEADS = 4
G_DK = 128
G_DV = 128
G_RANK = 16
G_TAU = 16.0
GLA_CHUNK = 128
M_INNER = 2 * D_MODEL
M_HEADS = 4
M_DH = M_INNER // M_HEADS
M_CONV = 4
M_QKV_BLOCK = 4
MLSTM_CHUNK = 256
D_FF = 4 * D_MODEL

LANES = 128
SUBLANES = 8
MXU_DIM = 256
VMEM_LIMIT = 56 * 1024 * 1024

EV_AQ, EV_IQ, EV_GQ, EV_GK, EV_GV, EV_GR = 0, 1, 2, 3, 4, 5
EV_KV, EV_KIW, EV_GLR = 24, 25, 26
EV_COLS_PADDED = 27 * LANES

INT_MIN = -2147483648
KEY_NEG_INF = INT_MIN - (-8388608)
MASK_BIAS = -2e30
M_INIT = -1e30
LOG2_E = 1.4426950408889634


def _cparams(sem):
    return pltpu.CompilerParams(dimension_semantics=sem, vmem_limit_bytes=VMEM_LIMIT)


def _layer_norm(z, g, b):
    mu = jnp.mean(z, axis=-1, keepdims=True)
    zc = z - mu
    var = jnp.mean(zc * zc, axis=-1, keepdims=True)
    return zc * lax.rsqrt(var + LN_EPS) * g + b


def _dot(a, b):
    return jnp.dot(a, b, preferred_element_type=F32)


def _dot_nt(a, b):
    return lax.dot_general(a, b, (((1,), (1,)), ((), ())), preferred_element_type=F32)


def _dot_tn(a, b):
    return lax.dot_general(a, b, (((0,), (0,)), ((), ())), preferred_element_type=F32)


def _split_bf16(x):
    hi = x.astype(BF16)
    lo = (x - hi.astype(F32)).astype(BF16)
    return jnp.concatenate([hi, lo], axis=-1)


def _log_sigmoid(x):
    return jnp.minimum(x, 0.0) - jnp.log1p(jnp.exp(-jnp.abs(x)))


def _matmul_kernel(x_ref, w_ref, o_ref):
    o_ref[...] = _dot(x_ref[...], w_ref[...]).astype(o_ref.dtype)


def _matmul(x, w, *, tm, tn, out_dtype):
    t, k = x.shape
    n = w.shape[1]
    return pl.pallas_call(
        _matmul_kernel,
        grid=(t // tm, n // tn),
        in_specs=[pl.BlockSpec((tm, k), lambda i, j: (i, 0)),
                  pl.BlockSpec((k, tn), lambda i, j: (0, j))],
        out_specs=pl.BlockSpec((tm, tn), lambda i, j: (i, j)),
        out_shape=jax.ShapeDtypeStruct((t, n), out_dtype),
        compiler_params=_cparams(("parallel", "arbitrary")),
        name="in_proj",
    )(x, w)


def _proj_ln_kernel(*refs, n_in):
    a_refs, w_refs = refs[:n_in], refs[n_in:2 * n_in]
    xf_ref, g_ref, b_ref, of_ref, ob_ref = refs[2 * n_in:]
    acc = _dot(a_refs[0][...], w_refs[0][...])
    for a_ref, w_ref in zip(a_refs[1:], w_refs[1:]):
        acc = acc + _dot(a_ref[...], w_ref[...])
    y = _layer_norm(ALPHA * xf_ref[...] + acc, g_ref[...], b_ref[...])
    of_ref[...] = y
    ob_ref[...] = y.astype(BF16)


def _proj_ln(acts, weights, xf, g, b, *, tm):
    t, d = xf.shape
    in_specs = [pl.BlockSpec((tm, a.shape[1]), lambda i: (i, 0)) for a in acts]
    in_specs += [pl.BlockSpec(w.shape, lambda i: (0, 0)) for w in weights]
    in_specs += [pl.BlockSpec((tm, d), lambda i: (i, 0)),
                 pl.BlockSpec((1, d), lambda i: (0, 0)),
                 pl.BlockSpec((1, d), lambda i: (0, 0))]
    return pl.pallas_call(
        functools.partial(_proj_ln_kernel, n_in=len(acts)),
        grid=(t // tm,),
        in_specs=in_specs,
        out_specs=[pl.BlockSpec((tm, d), lambda i: (i, 0)),
                   pl.BlockSpec((tm, d), lambda i: (i, 0))],
        out_shape=[jax.ShapeDtypeStruct((t, d), F32), jax.ShapeDtypeStruct((t, d), BF16)],
        compiler_params=_cparams(("parallel",)),
        name="out_proj_ln",
    )(*acts, *weights, xf, g, b)


def _mlp_kernel(xb_ref, xf_ref, w1_ref, w2_ref, g_ref, b_ref, of_ref, ob_ref, *, ff_chunk):
    xb = xb_ref[...]
    acc = None
    for c in range(D_FF // ff_chunk):
        h = _dot(xb, w1_ref[:, c * ff_chunk:(c + 1) * ff_chunk])
        h = jnp.maximum(h, 0.0)
        part = _dot((h * h).astype(BF16), w2_ref[c * ff_chunk:(c + 1) * ff_chunk, :])
        acc = part if acc is None else acc + part
    y = _layer_norm(ALPHA * xf_ref[...] + acc, g_ref[...], b_ref[...])
    of_ref[...] = y
    ob_ref[...] = y.astype(BF16)


def _mlp(xb, xf, w1, w2, g, b, *, tm, ff_chunk=1024):
    t, d = xf.shape
    const = lambda i: (0, 0)
    return pl.pallas_call(
        functools.partial(_mlp_kernel, ff_chunk=ff_chunk),
        grid=(t // tm,),
        in_specs=[pl.BlockSpec((tm, d), lambda i: (i, 0)),
                  pl.BlockSpec((tm, d), lambda i: (i, 0)),
                  pl.BlockSpec(w1.shape, const, pipeline_mode=pl.Buffered(1)),
                  pl.BlockSpec(w2.shape, const, pipeline_mode=pl.Buffered(1)),
                  pl.BlockSpec((1, d), const),
                  pl.BlockSpec((1, d), const)],
        out_specs=[pl.BlockSpec((tm, d), lambda i: (i, 0)),
                   pl.BlockSpec((tm, d), lambda i: (i, 0))],
        out_shape=[jax.ShapeDtypeStruct((t, d), F32), jax.ShapeDtypeStruct((t, d), BF16)],
        compiler_params=_cparams(("parallel",)),
        name="mlp_ln",
    )(xb, xf, w1, w2, g, b)


def _sortable_key(x):
    bits = pltpu.bitcast(x, I32)
    return jnp.where(bits < 0, INT_MIN - bits, bits)


def _column_count(hit):
    quarter = jnp.sum(hit.reshape(4, KEY_CHUNK // 4, LANES), axis=0)
    return jnp.sum(quarter.reshape(KEY_CHUNK // 4 // SUBLANES, SUBLANES, LANES), axis=0)


def _dsa_kernel(q_ref, qi_ref, kv_ref, kiw_ref, o_ref,
                qi_s, qa_s, skey_s, skey16_s, thr_s, bias_s, s_s, cmax_s, p_s, m_s, l_s, acc_s, *, nbatch, topk):
    i = pl.program_id(1)
    nch = i // 2 + 1
    key_in_chunk = lax.broadcasted_iota(I32, (KEY_CHUNK, Q_BLOCK), 0)
    q_in_block = lax.broadcasted_iota(I32, (KEY_CHUNK, Q_BLOCK), 1)
    nch_max = skey_s.shape[1]

    sel_r = lax.broadcasted_iota(I32, (2 * SUBLANES, LANES), 0)
    sel_c = lax.broadcasted_iota(I32, (2 * SUBLANES, LANES), 1)
    sel = jnp.where(sel_c == sel_r + IDX_DIM, 1.0, 0.0).astype(BF16)
    w_rows = []
    for g in range(nbatch):
        qi = qi_ref[g].astype(F32)
        q = q_ref[g].astype(F32) * (A_HEAD_DIM ** -0.5 * LOG2_E)
        for h in range(IDX_HEADS):
            qi_s[g, :, h * Q_BLOCK:(h + 1) * Q_BLOCK] = qi[:, h * IDX_DIM:(h + 1) * IDX_DIM].T.astype(BF16)
        for h in range(A_HEADS):
            qa_s[g, :, h * Q_BLOCK:(h + 1) * Q_BLOCK] = q[:, h * A_HEAD_DIM:(h + 1) * A_HEAD_DIM].T.astype(BF16)
        w_blk = kiw_ref[g, pl.ds(pl.multiple_of(i * Q_BLOCK, Q_BLOCK), Q_BLOCK), :]
        w_rows.append(_dot_nt(sel, w_blk)[:IDX_HEADS] * (IDX_HEADS ** -0.5 * IDX_DIM ** -0.5))
        m_s[g] = jnp.full(m_s.shape[1:], M_INIT, F32)
        l_s[g] = jnp.zeros(l_s.shape[1:], F32)
        acc_s[g] = jnp.zeros(acc_s.shape[1:], F32)

    npair = (nch + 1) // 2

    def chunk_start(c):
        return pl.multiple_of(jnp.minimum(c, nch - 1) * KEY_CHUNK, KEY_CHUNK)

    def store_heads(g, slot, res):
        for h in range(A_HEADS):
            s_s[g, slot, h] = res[:, h * Q_BLOCK:(h + 1) * Q_BLOCK]

    def logits_into(c, slot):
        for g in range(nbatch):
            ki = kiw_ref[g, pl.ds(chunk_start(c), KEY_CHUNK), :][:, :IDX_DIM]
            store_heads(g, slot, _dot(ki, qi_s[g]))

    def score_from(c, slot):
        valid = (chunk_start(c) + key_in_chunk) <= (i * Q_BLOCK + q_in_block)
        for g in range(nbatch):
            sc = None
            for h in range(IDX_HEADS):
                t = w_rows[g][h:h + 1, :] * jnp.maximum(s_s[g, slot, h], 0.0)
                sc = t if sc is None else sc + t
            key = _sortable_key(jnp.where(valid, sc, -jnp.inf))
            skey_s[g, jnp.minimum(c, nch - 1)] = key
            skey16_s[g, jnp.minimum(c, nch - 1)] = lax.shift_right_arithmetic(key, 16).astype(jnp.int16)

    def score_body(j, carry):
        logits_into(2 * j + 1, 1)
        score_from(2 * j, 0)
        logits_into(2 * j + 2, 0)
        score_from(2 * j + 1, 1)
        return carry

    logits_into(0, 0)
    lax.fori_loop(0, npair, score_body, 0)

    def count_where(pred):
        def body(c, accs):
            return tuple(accs[g] + _column_count(jnp.where(pred(g, c, skey_s[g, c]), 1, 0))
                         for g in range(nbatch))
        zero = tuple(jnp.zeros((SUBLANES, LANES), I32) for _ in range(nbatch))
        accs = lax.fori_loop(0, nch, body, zero)
        return [jnp.sum(a, axis=0, keepdims=True) for a in accs]

    for g in range(nbatch):
        thr_s[g] = jnp.full(thr_s.shape[1:], KEY_NEG_INF + 1, I32)

    def count_high_ge(cands):
        rows = 2 * SUBLANES
        cands16 = [jnp.broadcast_to(cd, (rows, LANES)).astype(jnp.int16) for cd in cands]

        def body(c, accs):
            out = []
            for g in range(nbatch):
                hi = skey16_s[g, c]
                parts = [jnp.where(hi[r * rows:(r + 1) * rows] >= cands16[g], jnp.bfloat16(1), jnp.bfloat16(0))
                         for r in range(KEY_CHUNK // rows)]
                while len(parts) > 1:
                    parts = [parts[a] + parts[a + 1] for a in range(0, len(parts), 2)]
                out.append(accs[g] + parts[0])
            return tuple(out)

        zero = tuple(jnp.zeros((rows, LANES), BF16) for _ in range(nbatch))
        accs = lax.fori_loop(0, nch, body, zero)
        return [jnp.sum(a.astype(F32), axis=0, keepdims=True) for a in accs]

    @pl.when(i * Q_BLOCK >= topk)
    def _search():
        def high_body(it, los):
            inc = lax.shift_left(jnp.int32(1), 15 - it)
            cands = [lo + inc for lo in los]
            cnt = count_high_ge(cands)
            return tuple(jnp.where(cnt[g] >= topk, cands[g], los[g]) for g in range(nbatch))

        highs = lax.fori_loop(0, 16, high_body,
                              tuple(jnp.full((1, LANES), -32768, I32) for _ in range(nbatch)))

        above = count_high_ge([hi + 1 for hi in highs])
        need_low = [topk - above[g] for g in range(nbatch)]
        rows = 2 * SUBLANES
        h16 = [jnp.broadcast_to(hi, (rows, LANES)).astype(jnp.int16) for hi in highs]

        def bucket_body(c, carry):
            for g in range(nbatch):
                low = (lax.bitwise_and(skey_s[g, c], 0xFFFF) - 32768).astype(jnp.int16)
                for r in range(KEY_CHUNK // rows):
                    rs = slice(r * rows, (r + 1) * rows)
                    skey16_s[g, c, rs, :] = jnp.where(skey16_s[g, c, rs, :] == h16[g], low[rs], jnp.int16(-32768))
            return carry

        lax.fori_loop(0, nch, bucket_body, 0)

        def low_body(it, los):
            inc = lax.shift_left(jnp.int32(1), 15 - it)
            cands = [lo + inc for lo in los]
            cnt = count_high_ge(cands)
            return tuple(jnp.where(cnt[g] >= need_low[g], cands[g], los[g]) for g in range(nbatch))

        lows = lax.fori_loop(0, 16, low_body,
                             tuple(jnp.full((1, LANES), -32768, I32) for _ in range(nbatch)))
        los = [lax.shift_left(highs[g], 16) + (lows[g] + 32768) for g in range(nbatch)]
        for g in range(nbatch):
            thr_s[g] = jnp.broadcast_to(los[g], thr_s.shape[1:])

        cnt_ge = count_where(lambda g, c, kk: kk >= los[g])
        worst = cnt_ge[0]
        for g in range(1, nbatch):
            worst = jnp.maximum(worst, cnt_ge[g])

        @pl.when(jnp.max(worst) > topk)
        def _ties():
            cnt_gt = count_where(lambda g, c, kk: kk > los[g])
            need = [topk - cnt_gt[g] for g in range(nbatch)]
            nbits = max(1, int(np.ceil(np.log2(nch_max * KEY_CHUNK))))

            def idx_body(it, ps):
                inc = lax.shift_left(jnp.int32(1), nbits - 1 - it)

                def pred(g, c, kk):
                    idx = c * KEY_CHUNK + key_in_chunk
                    return jnp.logical_and(kk == los[g], idx < ps[g] + inc)

                cnt = count_where(pred)
                return tuple(jnp.where(cnt[g] < need[g], ps[g] + inc, ps[g]) for g in range(nbatch))

            ps = lax.fori_loop(0, nbits, idx_body,
                               tuple(jnp.zeros((1, LANES), I32) for _ in range(nbatch)))

            def demote_body(c, carry):
                idx = c * KEY_CHUNK + key_in_chunk
                for g in range(nbatch):
                    kk = skey_s[g, c]
                    drop = jnp.logical_and(kk == los[g], idx > ps[g])
                    skey_s[g, c] = jnp.where(drop, INT_MIN, kk)
                return carry

            lax.fori_loop(0, nch, demote_body, 0)

    def qk_into(c, slot):
        cc = jnp.minimum(c, nch - 1)
        for g in range(nbatch):
            k = kv_ref[g, pl.ds(chunk_start(c), KEY_CHUNK), :][:, :A_HEAD_DIM]
            thr = jnp.where(c < nch, thr_s[g][0:1, :], jnp.int32(2147483647))
            bias_s[g, slot] = jnp.where(skey_s[g, cc] >= thr, 0.0, MASK_BIAS)
            res = _dot(k, qa_s[g])
            cmax = []
            for h in range(A_HEADS):
                sh = res[:, h * Q_BLOCK:(h + 1) * Q_BLOCK] + bias_s[g, slot]
                s_s[g, slot, h] = sh
                cmax.append(jnp.max(sh, axis=0, keepdims=True))
            cmax_s[g, slot] = jnp.broadcast_to(jnp.concatenate(cmax, axis=1), cmax_s.shape[2:])

    def softmax_from(c, slot):
        for g in range(nbatch):
            v = kv_ref[g, pl.ds(chunk_start(c), KEY_CHUNK), :][:, A_HEAD_DIM:]
            m_prev = m_s[g][0:1, :]
            m_new = jnp.maximum(m_prev, cmax_s[g, slot][0:1, :])
            alpha = jnp.exp2(m_prev - m_new)
            sums = []
            for h in range(A_HEADS):
                cs = slice(h * Q_BLOCK, (h + 1) * Q_BLOCK)
                p = jnp.exp2(s_s[g, slot, h] - m_new[:, cs])
                p_s[g, slot, :, cs] = p.astype(BF16)
                sums.append(jnp.sum(p, axis=0, keepdims=True))
            l_s[g] = jnp.broadcast_to(alpha * l_s[g][0:1, :] + jnp.concatenate(sums, axis=1), l_s.shape[1:])
            m_s[g] = jnp.broadcast_to(m_new, m_s.shape[1:])
            acc_s[g] = acc_s[g] * alpha + _dot_tn(v, p_s[g, slot])

    def attn_body(j, carry):
        qk_into(2 * j + 1, 1)
        softmax_from(2 * j, 0)
        qk_into(2 * j + 2, 0)
        softmax_from(2 * j + 1, 1)
        return carry

    qk_into(0, 0)
    lax.fori_loop(0, npair, attn_body, 0)

    for g in range(nbatch):
        out_t = acc_s[g] / l_s[g][0:1, :]
        outs = [out_t[:, h * Q_BLOCK:(h + 1) * Q_BLOCK].T for h in range(A_HEADS)]
        o_ref[g] = jnp.concatenate(outs, axis=1).astype(o_ref.dtype)


def _dsa(proj, *, nbatch):
    b, s, _ = proj.shape
    topk = min(TOPK_MAX, s // 4)
    assert topk % Q_BLOCK == 0 and s % KEY_CHUNK == 0 and b % nbatch == 0
    nb = s // Q_BLOCK
    nch_max = s // KEY_CHUNK
    width = A_HEADS * A_HEAD_DIM
    return pl.pallas_call(
        functools.partial(_dsa_kernel, nbatch=nbatch, topk=topk),
        grid=(b // nbatch, nb),
        in_specs=[pl.BlockSpec((nbatch, Q_BLOCK, width), lambda bi, i: (bi, i, EV_AQ)),
                  pl.BlockSpec((nbatch, Q_BLOCK, width), lambda bi, i: (bi, i, EV_IQ)),
                  pl.BlockSpec((nbatch, s, LANES), lambda bi, i: (bi, 0, EV_KV), pipeline_mode=pl.Buffered(1)),
                  pl.BlockSpec((nbatch, s, LANES), lambda bi, i: (bi, 0, EV_KIW), pipeline_mode=pl.Buffered(1))],
        out_specs=pl.BlockSpec((nbatch, Q_BLOCK, width), lambda bi, i: (bi, i, 0)),
        out_shape=jax.ShapeDtypeStruct((b, s, width), BF16),
        scratch_shapes=[
            pltpu.VMEM((nbatch, IDX_DIM, IDX_HEADS * Q_BLOCK), BF16),
            pltpu.VMEM((nbatch, A_HEAD_DIM, A_HEADS * Q_BLOCK), BF16),
            pltpu.VMEM((nbatch, nch_max, KEY_CHUNK, Q_BLOCK), I32),
            pltpu.VMEM((nbatch, nch_max, KEY_CHUNK, Q_BLOCK), jnp.int16),
            pltpu.VMEM((nbatch, SUBLANES, Q_BLOCK), I32),
            pltpu.VMEM((nbatch, 2, KEY_CHUNK, Q_BLOCK), F32),
            pltpu.VMEM((nbatch, 2, A_HEADS, KEY_CHUNK, Q_BLOCK), F32),
            pltpu.VMEM((nbatch, 2, SUBLANES, A_HEADS * Q_BLOCK), F32),
            pltpu.VMEM((nbatch, 2, KEY_CHUNK, A_HEADS * Q_BLOCK), BF16),
            pltpu.VMEM((nbatch, SUBLANES, A_HEADS * Q_BLOCK), F32),
            pltpu.VMEM((nbatch, SUBLANES, A_HEADS * Q_BLOCK), F32),
            pltpu.VMEM((nbatch, A_HEAD_DIM, A_HEADS * Q_BLOCK), F32),
        ],
        compiler_params=_cparams(("parallel", "arbitrary")),
        name="dsa",
    )(proj, proj, proj, proj)


def _gla_constants(chunk):
    t = np.arange(chunk)[:, None]
    r = np.arange(chunk)[None, :]
    blocks = [(r <= t), (r > t)]
    masks = [(t == r)]
    n = chunk // 2
    while n >= 1:
        upper = (t // n) % 2 == 1
        mid = (t // n) * n
        end = (t // n + 1) * n - 1
        blocks.append((upper & (r >= mid) & (r <= t)) | (~upper & (r >= t + 1) & (r <= end)))
        s = r
        masks.append(upper & ((s // n) % 2 == 0) & (t // (2 * n) == s // (2 * n)))
        n //= 2
    return (np.concatenate(blocks, axis=0).astype(np.float32),
            np.stack(masks, axis=0).astype(np.float32))


def _gla_kernel(q_ref, k_ref, v_ref, r_ref, lr_ref, w2_ref, b2_ref, gn_ref, sums_ref, masks_ref,
                o_ref, state_s, e_s, *, chunk):
    nlev = masks_ref.shape[0] - 1

    @pl.when(pl.program_id(1) == 0)
    def _init():
        state_s[...] = jnp.zeros_like(state_s)

    x = _dot(lr_ref[0][:, :G_RANK], w2_ref[...]) + b2_ref[...]
    la = _log_sigmoid(x) * (1.0 / G_TAU)
    hi = la.astype(BF16)
    lo = (la - hi.astype(F32)).astype(BF16)
    dec = _dot(sums_ref[...], jnp.concatenate([hi, lo], axis=0))
    e_s[...] = jnp.exp(dec)

    for h in range(G_HEADS):
        cs = slice(h * G_DK, (h + 1) * G_DK)
        qf = q_ref[0][:, cs].astype(F32) * G_DK ** -0.5
        kb = k_ref[0][:, cs]
        kf = kb.astype(F32)
        vb = v_ref[0][:, cs]
        st = state_s[h]

        o = _dot_nt((qf * e_s[0:chunk, cs]).astype(BF16), st.astype(BF16))
        att = masks_ref[0] * _dot_nt(qf.astype(BF16), kb)
        for lv in range(nlev):
            e_lv = e_s[(2 + lv) * chunk:(3 + lv) * chunk, cs]
            att = att + masks_ref[lv + 1] * _dot_nt((qf * e_lv).astype(BF16), (kf * e_lv).astype(BF16))
        o = o + _dot(att.astype(BF16), vb)

        kd = (kf * e_s[chunk:2 * chunk, cs]).astype(BF16)
        state_s[h] = st * e_s[chunk - 1:chunk, cs] + _dot_tn(vb, kd)

        mu = jnp.mean(o, axis=-1, keepdims=True)
        oc = o - mu
        var = jnp.mean(oc * oc, axis=-1, keepdims=True)
        on = oc * lax.rsqrt(var + LN_EPS) * gn_ref[h:h + 1, :]
        rg = r_ref[0][:, cs].astype(F32)
        o_ref[0, :, cs] = (on * (rg * jax.nn.sigmoid(rg))).astype(o_ref.dtype)


def _gla(proj, w2, b2, gnorm, *, chunk=GLA_CHUNK):
    b, s, _ = proj.shape
    width = G_HEADS * G_DK
    sums, masks = _gla_constants(chunk)
    sums = jnp.asarray(np.concatenate([sums, sums], axis=1), BF16)
    masks = jnp.asarray(masks, F32)
    col = lambda blk: pl.BlockSpec((1, chunk, width), lambda bi, c: (bi, c, blk))
    const2 = lambda bi, c: (0, 0)
    return pl.pallas_call(
        functools.partial(_gla_kernel, chunk=chunk),
        grid=(b, s // chunk),
        in_specs=[col(EV_GQ), col(EV_GK), col(EV_GV), col(EV_GR),
                  pl.BlockSpec((1, chunk, LANES), lambda bi, c: (bi, c, EV_GLR)),
                  pl.BlockSpec(w2.shape, const2),
                  pl.BlockSpec(b2.shape, const2),
                  pl.BlockSpec(gnorm.shape, const2),
                  pl.BlockSpec(sums.shape, const2),
                  pl.BlockSpec(masks.shape, lambda bi, c: (0, 0, 0))],
        out_specs=pl.BlockSpec((1, chunk, width), lambda bi, c: (bi, c, 0)),
        out_shape=jax.ShapeDtypeStruct((b, s, width), BF16),
        scratch_shapes=[pltpu.VMEM((G_HEADS, G_DV, G_DK), F32),
                        pltpu.VMEM((sums.shape[0], width), F32)],
        compiler_params=_cparams(("parallel", "arbitrary")),
        name="gla",
    )(proj, proj, proj, proj, proj, w2, b2, gnorm, sums, masks)


def _mlstm_pre_kernel(x_ref, win_ref, cw_ref, cb_ref, wq_ref, wk_ref, wv_ref, wg_ref, bg_ref,
                      q_ref, k_ref, v_ref, xc_ref, op_ref, gate_ref, ext_s, *, tm):
    @pl.when(pl.program_id(1) == 0)
    def _init():
        ext_s[0:SUBLANES, :] = jnp.zeros((SUBLANES, M_INNER), F32)

    xb = x_ref[0]
    ext_s[SUBLANES:, :] = _dot(xb, win_ref[:, :M_INNER])
    op_ref[0] = _dot(xb, win_ref[:, M_INNER:]).astype(op_ref.dtype)
    gates = bg_ref[...]
    for gi in range(M_INNER // MXU_DIM):
        cs = slice(gi * MXU_DIM, (gi + 1) * MXU_DIM)
        conv = cb_ref[:, cs]
        for j in range(M_CONV):
            off = SUBLANES - (M_CONV - 1) + j
            conv = conv + cw_ref[j:j + 1, cs] * ext_s[off:off + tm, cs]
        x_c = conv * jax.nn.sigmoid(conv)
        xcb = x_c.astype(BF16)
        xc_ref[0, :, cs] = xcb
        xmb = ext_s[SUBLANES:, cs].astype(BF16)
        for src, w_ref, dst, part in ((xcb, wq_ref, q_ref, 0), (xcb, wk_ref, k_ref, 1), (xmb, wv_ref, v_ref, 2)):
            y = _dot(src, w_ref[gi]).astype(BF16)
            dst[0, :, cs] = y
            gates = gates + _dot(y, wg_ref[part * M_INNER + gi * MXU_DIM:part * M_INNER + (gi + 1) * MXU_DIM, :])
    ext_s[0:SUBLANES, :] = ext_s[tm:tm + SUBLANES, :]
    gate_ref[0] = gates[:, :2 * M_HEADS]


def _mlstm_pre(xb, win, cw, cb, wq, wk, wv, wg, bg, *, tm):
    b, s, d = xb.shape
    const2 = lambda bi, t: (0, 0)
    const3 = lambda bi, t: (0, 0, 0)
    tile = lambda dt: jax.ShapeDtypeStruct((b, s, M_INNER), dt)
    ospec = pl.BlockSpec((1, tm, M_INNER), lambda bi, t: (bi, t, 0))
    return pl.pallas_call(
        functools.partial(_mlstm_pre_kernel, tm=tm),
        grid=(b, s // tm),
        in_specs=[pl.BlockSpec((1, tm, d), lambda bi, t: (bi, t, 0)),
                  pl.BlockSpec(win.shape, const2, pipeline_mode=pl.Buffered(1)),
                  pl.BlockSpec(cw.shape, const2),
                  pl.BlockSpec(cb.shape, const2),
                  pl.BlockSpec(wq.shape, const3),
                  pl.BlockSpec(wk.shape, const3),
                  pl.BlockSpec(wv.shape, const3),
                  pl.BlockSpec(wg.shape, const2),
                  pl.BlockSpec(bg.shape, const2)],
        out_specs=[ospec, ospec, ospec, ospec, ospec,
                   pl.BlockSpec((1, tm, 2 * M_HEADS), lambda bi, t: (bi, t, 0))],
        out_shape=[tile(BF16), tile(BF16), tile(BF16), tile(BF16), tile(BF16),
                   jax.ShapeDtypeStruct((b, s, 2 * M_HEADS), F32)],
        scratch_shapes=[pltpu.VMEM((tm + SUBLANES, M_INNER), F32)],
        compiler_params=_cparams(("parallel", "arbitrary")),
        name="mlstm_pre",
    )(xb, win, cw, cb, wq, wk, wv, wg, bg)


def _mlstm_kernel(q_ref, k_ref, v_ref, xc_ref, op_ref, gc_ref, gr_ref, skip_ref, ng_ref, tri_ref, ones_ref,
                  y_ref, c_s, n_s, m_s, num_s, rden_s, *, chunk):
    @pl.when(pl.program_id(1) == 0)
    def _init():
        c_s[...] = jnp.zeros_like(c_s)
        n_s[...] = jnp.zeros_like(n_s)
        m_s[...] = jnp.zeros_like(m_s)

    tri = tri_ref[...]
    t_idx = lax.broadcasted_iota(I32, (chunk, chunk), 0)
    s_idx = lax.broadcasted_iota(I32, (chunk, chunk), 1)
    causal = s_idx <= t_idx
    gc = gc_ref[0]
    gr = gr_ref[0]
    for h in range(M_HEADS):
        cs = slice(h * M_DH, (h + 1) * M_DH)
        li_c = gc[:, h:h + 1]
        lf_c = _log_sigmoid(gc[:, M_HEADS + h:M_HEADS + h + 1])
        li_r = gr[h:h + 1, :]
        lf_r = _log_sigmoid(gr[M_HEADS + h:M_HEADS + h + 1, :])
        bc2 = _dot(tri, _split_bf16(jnp.broadcast_to(lf_c, (chunk, LANES))))
        b_c = (bc2[:, :LANES] + bc2[:, LANES:])[:, 0:1]
        lf_r8 = jnp.broadcast_to(lf_r, (SUBLANES, chunk))
        hi = lf_r8.astype(BF16)
        lo = (lf_r8 - hi.astype(F32)).astype(BF16)
        br2 = _dot_nt(jnp.concatenate([hi, lo], axis=0), tri)
        b_r = br2[0:1, :] + br2[SUBLANES:SUBLANES + 1, :]

        m_prev = m_s[h][0:1, 0:1]
        dmat = jnp.where(causal, b_c - b_r + li_r, -jnp.inf)
        inter = b_c + m_prev
        m_t = jnp.maximum(inter, jnp.max(dmat, axis=1, keepdims=True))
        w_inter = jnp.exp(inter - m_t)

        qb = q_ref[0, :, cs]
        kb = (k_ref[0, :, cs].astype(F32) * M_DH ** -0.5).astype(BF16)
        vb = v_ref[0, :, cs]
        n_prev = n_s[h][0:1, :]

        pb = (jnp.exp(dmat - m_t) * _dot_nt(qb, kb)).astype(BF16)
        num_s[...] = w_inter * _dot(qb, c_s[h].astype(BF16)) + _dot(pb, vb)
        qn = _dot_nt(qb, n_s[h].astype(BF16))[:, 0:1]
        den = w_inter * qn + _dot(pb, ones_ref[...])[:, 0:1]
        rden_s[...] = jnp.broadcast_to(1.0 / jnp.maximum(jnp.abs(den), jnp.exp(-m_t)), rden_s.shape)

        g_tot = b_c[chunk - 1:chunk, :]
        d_end = g_tot - b_c + li_c
        m_new = jnp.maximum(g_tot + m_prev, jnp.max(d_end, axis=0, keepdims=True))
        wk = jnp.exp(d_end - m_new)
        decay = jnp.exp(g_tot + m_prev - m_new)
        kw = k_ref[0, :, cs].astype(F32) * (wk * M_DH ** -0.5)
        kwb = kw.astype(BF16)
        n_s[h] = jnp.broadcast_to(decay * n_prev + jnp.sum(kw, axis=0, keepdims=True), n_s.shape[1:])
        m_s[h] = jnp.broadcast_to(m_new, m_s.shape[1:])
        for r in range(M_DH // LANES):
            rs = slice(r * LANES, (r + 1) * LANES)
            c_s[h, rs, :] = decay * c_s[h, rs, :] + _dot_tn(kwb[:, rs], vb)

        slab = min(chunk, 64)
        for r in range(chunk // slab):
            rs = slice(r * slab, (r + 1) * slab)
            hg = num_s[rs, :] * rden_s[rs, 0:1] * jax.nn.sigmoid(op_ref[0, rs, cs].astype(F32))
            mu = jnp.mean(hg, axis=-1, keepdims=True)
            hc = hg - mu
            var = jnp.mean(hc * hc, axis=-1, keepdims=True)
            hn = hc * lax.rsqrt(var + LN_EPS) * ng_ref[h:h + 1, :]
            y_ref[0, rs, cs] = (hn + skip_ref[:, cs] * xc_ref[0, rs, cs].astype(F32)).astype(y_ref.dtype)


def _mlstm(q, k, v, xc, op, gates, skip, norm_g, *, chunk=MLSTM_CHUNK):
    b, s, _ = q.shape
    gates_t = jnp.swapaxes(gates, 1, 2)
    tri = jnp.asarray(np.tril(np.ones((chunk, chunk), np.float32)), BF16)
    ones = jnp.ones((chunk, LANES), BF16)
    big = pl.BlockSpec((1, chunk, M_INNER), lambda bi, c: (bi, c, 0))
    const2 = lambda bi, c: (0, 0)
    return pl.pallas_call(
        functools.partial(_mlstm_kernel, chunk=chunk),
        grid=(b, s // chunk),
        in_specs=[big, big, big, big, big,
                  pl.BlockSpec((1, chunk, 2 * M_HEADS), lambda bi, c: (bi, c, 0)),
                  pl.BlockSpec((1, 2 * M_HEADS, chunk), lambda bi, c: (bi, 0, c)),
                  pl.BlockSpec(skip.shape, const2),
                  pl.BlockSpec(norm_g.shape, const2),
                  pl.BlockSpec(tri.shape, const2),
                  pl.BlockSpec(ones.shape, const2)],
        out_specs=big,
        out_shape=jax.ShapeDtypeStruct((b, s, M_INNER), BF16),
        scratch_shapes=[pltpu.VMEM((M_HEADS, M_DH, M_DH), F32),
                        pltpu.VMEM((M_HEADS, SUBLANES, M_DH), F32),
                        pltpu.VMEM((M_HEADS, SUBLANES, LANES), F32),
                        pltpu.VMEM((chunk, M_DH), F32),
                        pltpu.VMEM((chunk, LANES), F32)],
        compiler_params=_cparams(("parallel", "arbitrary")),
        name="mlstm",
    )(q, k, v, xc, op, gates, gates_t, skip, norm_g, tri, ones)


def _even_w_in(w):
    sizes = (512, 64, 64, 512, 64, 8, 512, 512, 512, 512, 16)
    offs = np.concatenate([[0], np.cumsum(sizes)])
    part = lambda j: w[:, offs[j]:offs[j + 1]]
    a_q, a_k, a_v, i_q, i_k, i_w, g_q, g_k, g_v, g_r, g_lr = (part(j) for j in range(11))
    zeros = lambda n: jnp.zeros((w.shape[0], n), w.dtype)
    cols = [a_q, i_q, g_q, g_k, g_v, g_r, a_k, a_v, i_k, i_w, zeros(LANES - 72), g_lr, zeros(LANES - G_RANK)]
    return jnp.concatenate(cols, axis=1).astype(BF16)


def _block_diag(w):
    per = MXU_DIM // M_QKV_BLOCK
    wg = w.reshape(-1, per, M_QKV_BLOCK, M_QKV_BLOCK)
    eye = jnp.eye(per, dtype=w.dtype)
    dense = jnp.einsum('gade,ab->gadbe', wg, eye)
    return dense.reshape(-1, MXU_DIM, MXU_DIM).astype(BF16)


def _token_tile(t):
    for tm in (512, 256, 128):
        if t % tm == 0:
            return tm
    raise ValueError(f"token count {t} is not a multiple of 128")


def kernel(x, ev_w_in, ev_g_w2, ev_g_b2, ev_g_norm, ev_w_out, od_w_in, od_conv_w, od_conv_b, od_w_q, od_w_k, od_w_v, od_w_gate, od_b_gate, od_skip, od_norm, od_w_out, ln1_g, ln1_b, ln2_g, ln2_b, mlp_w1, mlp_w2):
    b, s, d = x.shape
    t = b * s
    tm = _token_tile(t)
    xf = x.reshape(t, d)
    xb = xf.astype(BF16)
    row = lambda v: v.reshape(1, -1)
    a_width = A_HEADS * A_HEAD_DIM
    for l in range(DEPTH):
        j = l // 2
        if l % 2 == 0:
            proj = _matmul(xb, _even_w_in(ev_w_in[j]), tm=tm, tn=EV_COLS_PADDED // 3, out_dtype=BF16)
            proj = proj.reshape(b, s, EV_COLS_PADDED)
            a_out = _dsa(proj, nbatch=next(n for n in (4, 2, 1) if b % n == 0))
            g_out = _gla(proj, ev_g_w2[j].astype(BF16), row(ev_g_b2[j]), ev_g_norm[j])
            w_out = ev_w_out[j].astype(BF16)
            acts = [a_out.reshape(t, a_width), g_out.reshape(t, -1)]
            weights = [w_out[:a_width], w_out[a_width:]]
        else:
            wg = jnp.pad(od_w_gate[j], ((0, 0), (0, LANES - 2 * M_HEADS))).astype(BF16)
            bg = jnp.pad(od_b_gate[j], (0, LANES - 2 * M_HEADS)).reshape(1, LANES)
            q, k, v, xc, op, gates = _mlstm_pre(
                xb.reshape(b, s, d), od_w_in[j].astype(BF16), od_conv_w[j], row(od_conv_b[j]),
                _block_diag(od_w_q[j]), _block_diag(od_w_k[j]), _block_diag(od_w_v[j]), wg, bg,
                tm=min(256, s))
            y = _mlstm(q, k, v, xc, op, gates, row(od_skip[j]), od_norm[j], chunk=min(MLSTM_CHUNK, s))
            acts = [y.reshape(t, M_INNER)]
            weights = [od_w_out[j].astype(BF16)]
        xf, xb = _proj_ln(acts, weights, xf, row(ln1_g[l]), row(ln1_b[l]), tm=tm)
        xf, xb = _mlp(xb, xf, mlp_w1[l].astype(BF16), mlp_w2[l].astype(BF16), row(ln2_g[l]), row(ln2_b[l]), tm=tm)
    return xf.reshape(b, s, d)
```

```python
import functools

import numpy as np
import jax
import jax.numpy as jnp
from jax import lax
from jax.experimental import pallas as pl
from jax.experimental.pallas import tpu as pltpu

F32 = jnp.float32
BF16 = jnp.bfloat16
I32 = jnp.int32

D_MODEL = 1024
DEPTH = 4
ALPHA = (2 * DEPTH) ** 0.25
LN_EPS = 1e-5
A_HEADS = 8
A_HEAD_DIM = 64
IDX_HEADS = 8
IDX_DIM = 64
TOPK_MAX = 256
Q_BLOCK = 128
KEY_CHUNK = 256
G_HEADS = 4
G_DK = 128
G_DV = 128
G_RANK = 16
G_TAU = 16.0
GLA_CHUNK = 128
M_INNER = 2 * D_MODEL
M_HEADS = 4
M_DH = M_INNER // M_HEADS
M_CONV = 4
M_QKV_BLOCK = 4
MLSTM_CHUNK = 256
D_FF = 4 * D_MODEL

LANES = 128
SUBLANES = 8
MXU_DIM = 256
VMEM_LIMIT = 56 * 1024 * 1024

EV_AQ, EV_IQ, EV_GQ, EV_GK, EV_GV, EV_GR = 0, 1, 2, 3, 4, 5
EV_KV, EV_KIW, EV_GLR = 24, 25, 26
EV_COLS_PADDED = 27 * LANES

INT_MIN = -2147483648
KEY_NEG_INF = INT_MIN - (-8388608)
MASK_BIAS = -2e30
M_INIT = -1e30
LOG2_E = 1.4426950408889634


def _cparams(sem):
    return pltpu.CompilerParams(dimension_semantics=sem, vmem_limit_bytes=VMEM_LIMIT)


def _layer_norm(z, g, b):
    mu = jnp.mean(z, axis=-1, keepdims=True)
    zc = z - mu
    var = jnp.mean(zc * zc, axis=-1, keepdims=True)
    return zc * lax.rsqrt(var + LN_EPS) * g + b


def _dot(a, b):
    return jnp.dot(a, b, preferred_element_type=F32)


def _dot_nt(a, b):
    return lax.dot_general(a, b, (((1,), (1,)), ((), ())), preferred_element_type=F32)


def _dot_tn(a, b):
    return lax.dot_general(a, b, (((0,), (0,)), ((), ())), preferred_element_type=F32)


def _split_bf16(x):
    hi = x.astype(BF16)
    lo = (x - hi.astype(F32)).astype(BF16)
    return jnp.concatenate([hi, lo], axis=-1)


def _log_sigmoid(x):
    return jnp.minimum(x, 0.0) - jnp.log1p(jnp.exp(-jnp.abs(x)))


def _matmul_kernel(x_ref, w_ref, o_ref, *, tn):
    x = x_ref[...]
    for j in range(w_ref.shape[1] // tn):
        o_ref[:, j * tn:(j + 1) * tn] = _dot(x, w_ref[:, j * tn:(j + 1) * tn]).astype(o_ref.dtype)


def _matmul(x, w, *, tm, tn, out_dtype):
    t, k = x.shape
    n = w.shape[1]
    assert n % tn == 0
    return pl.pallas_call(
        functools.partial(_matmul_kernel, tn=tn),
        grid=(t // tm,),
        in_specs=[pl.BlockSpec((tm, k), lambda i: (i, 0)),
                  pl.BlockSpec((k, n), lambda i: (0, 0), pipeline_mode=pl.Buffered(1))],
        out_specs=pl.BlockSpec((tm, n), lambda i: (i, 0)),
        out_shape=jax.ShapeDtypeStruct((t, n), out_dtype),
        compiler_params=_cparams(("parallel",)),
        name="in_proj",
    )(x, w)


def _layer_tail_kernel(*refs, n_in, ff_chunk):
    a_refs, wo_refs = refs[:n_in], refs[n_in:2 * n_in]
    xf_ref, g1_ref, b1_ref, w1_ref, w2_ref, g2_ref, b2_ref, of_ref, ob_ref = refs[2 * n_in:]
    acc = _dot(a_refs[0][...], wo_refs[0][...])
    for a_ref, w_ref in zip(a_refs[1:], wo_refs[1:]):
        acc = acc + _dot(a_ref[...], w_ref[...])
    x1 = _layer_norm(ALPHA * xf_ref[...] + acc, g1_ref[...], b1_ref[...])
    xb = x1.astype(BF16)
    acc = None
    for c in range(D_FF // ff_chunk):
        h = _dot(xb, w1_ref[:, c * ff_chunk:(c + 1) * ff_chunk])
        h = jnp.maximum(h, 0.0)
        part = _dot((h * h).astype(BF16), w2_ref[c * ff_chunk:(c + 1) * ff_chunk, :])
        acc = part if acc is None else acc + part
    y = _layer_norm(ALPHA * x1 + acc, g2_ref[...], b2_ref[...])
    of_ref[...] = y
    ob_ref[...] = y.astype(BF16)


def _layer_tail(acts, wos, xf, g1, b1, w1, w2, g2, b2, *, tm, ff_chunk=1024):
    t, d = xf.shape
    const = lambda i: (0, 0)
    resident = lambda w: pl.BlockSpec(w.shape, const, pipeline_mode=pl.Buffered(1))
    vec = pl.BlockSpec((1, d), const)
    tile = pl.BlockSpec((tm, d), lambda i: (i, 0))
    in_specs = [pl.BlockSpec((tm, a.shape[1]), lambda i: (i, 0)) for a in acts]
    in_specs += [resident(w) for w in wos]
    in_specs += [tile, vec, vec, resident(w1), resident(w2), vec, vec]
    return pl.pallas_call(
        functools.partial(_layer_tail_kernel, n_in=len(acts), ff_chunk=ff_chunk),
        grid=(t // tm,),
        in_specs=in_specs,
        out_specs=[tile, tile],
        out_shape=[jax.ShapeDtypeStruct((t, d), F32), jax.ShapeDtypeStruct((t, d), BF16)],
        compiler_params=_cparams(("parallel",)),
        name="proj_ln_mlp_ln",
    )(*acts, *wos, xf, g1, b1, w1, w2, g2, b2)


def _sortable_key(x):
    bits = pltpu.bitcast(x, I32)
    return jnp.where(bits < 0, INT_MIN - bits, bits)


def _column_count(hit):
    quarter = jnp.sum(hit.reshape(4, KEY_CHUNK // 4, LANES), axis=0)
    return jnp.sum(quarter.reshape(KEY_CHUNK // 4 // SUBLANES, SUBLANES, LANES), axis=0)


def _dsa_kernel(q_ref, qi_ref, kv_ref, kiw_ref, o_ref,
                qi_s, qa_s, skey_s, skey16_s, thr_s, bias_s, s_s, cmax_s, p_s, m_s, l_s, acc_s, *, nbatch, topk):
    i = pl.program_id(1)
    nch = i // 2 + 1
    key_in_chunk = lax.broadcasted_iota(I32, (KEY_CHUNK, Q_BLOCK), 0)
    q_in_block = lax.broadcasted_iota(I32, (KEY_CHUNK, Q_BLOCK), 1)
    nch_max = skey_s.shape[1]

    sel_r = lax.broadcasted_iota(I32, (2 * SUBLANES, LANES), 0)
    sel_c = lax.broadcasted_iota(I32, (2 * SUBLANES, LANES), 1)
    sel = jnp.where(sel_c == sel_r + IDX_DIM, 1.0, 0.0).astype(BF16)
    w_rows = []
    for g in range(nbatch):
        qi = qi_ref[g].astype(F32)
        q = q_ref[g].astype(F32) * (A_HEAD_DIM ** -0.5 * LOG2_E)
        for h in range(IDX_HEADS):
            qi_s[g, :, h * Q_BLOCK:(h + 1) * Q_BLOCK] = qi[:, h * IDX_DIM:(h + 1) * IDX_DIM].T.astype(BF16)
        for h in range(A_HEADS):
            qa_s[g, :, h * Q_BLOCK:(h + 1) * Q_BLOCK] = q[:, h * A_HEAD_DIM:(h + 1) * A_HEAD_DIM].T.astype(BF16)
        w_blk = kiw_ref[g, pl.ds(pl.multiple_of(i * Q_BLOCK, Q_BLOCK), Q_BLOCK), :]
        w_rows.append(_dot_nt(sel, w_blk)[:IDX_HEADS] * (IDX_HEADS ** -0.5 * IDX_DIM ** -0.5))
        m_s[g] = jnp.full(m_s.shape[1:], M_INIT, F32)
        l_s[g] = jnp.zeros(l_s.shape[1:], F32)
        acc_s[g] = jnp.zeros(acc_s.shape[1:], F32)

    npair = (nch + 1) // 2

    def chunk_start(c):
        return pl.multiple_of(jnp.minimum(c, nch - 1) * KEY_CHUNK, KEY_CHUNK)

    def store_heads(g, slot, res):
        for h in range(A_HEADS):
            s_s[g, slot, h] = res[:, h * Q_BLOCK:(h + 1) * Q_BLOCK]

    def logits_into(c, slot):
        for g in range(nbatch):
            ki = kiw_ref[g, pl.ds(chunk_start(c), KEY_CHUNK), :][:, :IDX_DIM]
            store_heads(g, slot, _dot(ki, qi_s[g]))

    def score_from(c, slot):
        valid = (c * KEY_CHUNK + key_in_chunk) <= (i * Q_BLOCK + q_in_block)
        for g in range(nbatch):
            sc = None
            for h in range(IDX_HEADS):
                t = w_rows[g][h:h + 1, :] * jnp.maximum(s_s[g, slot, h], 0.0)
                sc = t if sc is None else sc + t
            key = _sortable_key(jnp.where(valid, sc, -jnp.inf))
            skey_s[g, c] = key
            skey16_s[g, c] = lax.shift_right_arithmetic(key, 16).astype(jnp.int16)

    def score_body(j, carry):
        logits_into(2 * j + 1, 1)
        score_from(2 * j, 0)
        logits_into(2 * j + 2, 0)
        score_from(2 * j + 1, 1)
        return carry

    logits_into(0, 0)
    lax.fori_loop(0, npair, score_body, 0)

    def count_where(pred):
        def body(c, accs):
            return tuple(accs[g] + _column_count(jnp.where(pred(g, c, skey_s[g, c]), 1, 0))
                         for g in range(nbatch))
        zero = tuple(jnp.zeros((SUBLANES, LANES), I32) for _ in range(nbatch))
        accs = lax.fori_loop(0, nch, body, zero)
        return [jnp.sum(a, axis=0, keepdims=True) for a in accs]

    for g in range(nbatch):
        thr_s[g] = jnp.full(thr_s.shape[1:], KEY_NEG_INF + 1, I32)

    def count_high_ge(cands):
        rows = 2 * SUBLANES
        cands16 = [jnp.broadcast_to(cd, (rows, LANES)).astype(jnp.int16) for cd in cands]

        def body(j, accs):
            out = []
            for g in range(nbatch):
                parts = []
                for c in (2 * j, 2 * j + 1):
                    hi = skey16_s[g, c]
                    parts += [jnp.where(hi[r * rows:(r + 1) * rows] >= cands16[g], jnp.bfloat16(1), jnp.bfloat16(0))
                              for r in range(KEY_CHUNK // rows)]
                while len(parts) > 1:
                    parts = [parts[a] + parts[a + 1] for a in range(0, len(parts), 2)]
                out.append(accs[g] + parts[0])
            return tuple(out)

        zero = tuple(jnp.zeros((rows, LANES), BF16) for _ in range(nbatch))
        accs = lax.fori_loop(0, npair, body, zero)
        return [jnp.sum(a.astype(F32), axis=0, keepdims=True) for a in accs]

    @pl.when(i * Q_BLOCK >= topk)
    def _search():
        def high_body(it, los):
            inc = lax.shift_left(jnp.int32(1), 15 - it)
            cands = [lo + inc for lo in los]
            cnt = count_high_ge(cands)
            return tuple(jnp.where(cnt[g] >= topk, cands[g], los[g]) for g in range(nbatch))

        highs = lax.fori_loop(0, 16, high_body,
                              tuple(jnp.full((1, LANES), -32768, I32) for _ in range(nbatch)))

        above = count_high_ge([hi + 1 for hi in highs])
        need_low = [topk - above[g] for g in range(nbatch)]
        rows = 2 * SUBLANES
        h16 = [jnp.broadcast_to(hi, (rows, LANES)).astype(jnp.int16) for hi in highs]

        def bucket_body(c, carry):
            for g in range(nbatch):
                low = (lax.bitwise_and(skey_s[g, c], 0xFFFF) - 32768).astype(jnp.int16)
                for r in range(KEY_CHUNK // rows):
                    rs = slice(r * rows, (r + 1) * rows)
                    skey16_s[g, c, rs, :] = jnp.where(skey16_s[g, c, rs, :] == h16[g], low[rs], jnp.int16(-32768))
            return carry

        lax.fori_loop(0, 2 * npair, bucket_body, 0)

        def low_body(it, los):
            inc = lax.shift_left(jnp.int32(1), 15 - it)
            cands = [lo + inc for lo in los]
            cnt = count_high_ge(cands)
            return tuple(jnp.where(cnt[g] >= need_low[g], cands[g], los[g]) for g in range(nbatch))

        lows = lax.fori_loop(0, 16, low_body,
                             tuple(jnp.full((1, LANES), -32768, I32) for _ in range(nbatch)))
        los = [lax.shift_left(highs[g], 16) + (lows[g] + 32768) for g in range(nbatch)]
        for g in range(nbatch):
            thr_s[g] = jnp.broadcast_to(los[g], thr_s.shape[1:])

        cnt_ge = count_where(lambda g, c, kk: kk >= los[g])
        worst = cnt_ge[0]
        for g in range(1, nbatch):
            worst = jnp.maximum(worst, cnt_ge[g])

        @pl.when(jnp.max(worst) > topk)
        def _ties():
            cnt_gt = count_where(lambda g, c, kk: kk > los[g])
            need = [topk - cnt_gt[g] for g in range(nbatch)]
            nbits = max(1, int(np.ceil(np.log2(nch_max * KEY_CHUNK))))

            def idx_body(it, ps):
                inc = lax.shift_left(jnp.int32(1), nbits - 1 - it)

                def pred(g, c, kk):
                    idx = c * KEY_CHUNK + key_in_chunk
                    return jnp.logical_and(kk == los[g], idx < ps[g] + inc)

                cnt = count_where(pred)
                return tuple(jnp.where(cnt[g] < need[g], ps[g] + inc, ps[g]) for g in range(nbatch))

            ps = lax.fori_loop(0, nbits, idx_body,
                               tuple(jnp.zeros((1, LANES), I32) for _ in range(nbatch)))

            def demote_body(c, carry):
                idx = c * KEY_CHUNK + key_in_chunk
                for g in range(nbatch):
                    kk = skey_s[g, c]
                    drop = jnp.logical_and(kk == los[g], idx > ps[g])
                    skey_s[g, c] = jnp.where(drop, INT_MIN, kk)
                return carry

            lax.fori_loop(0, nch, demote_body, 0)

    def qk_into(c, slot):
        cc = jnp.minimum(c, nch - 1)
        for g in range(nbatch):
            k = kv_ref[g, pl.ds(chunk_start(c), KEY_CHUNK), :][:, :A_HEAD_DIM]
            thr = jnp.where(c < nch, thr_s[g][0:1, :], jnp.int32(2147483647))
            bias_s[g, slot] = jnp.where(skey_s[g, cc] >= thr, 0.0, MASK_BIAS)
            res = _dot(k, qa_s[g])
            cmax = []
            for h in range(A_HEADS):
                sh = res[:, h * Q_BLOCK:(h + 1) * Q_BLOCK] + bias_s[g, slot]
                s_s[g, slot, h] = sh
                cmax.append(jnp.max(sh, axis=0, keepdims=True))
            cmax_s[g, slot] = jnp.broadcast_to(jnp.concatenate(cmax, axis=1), cmax_s.shape[2:])

    def softmax_from(c, slot):
        for g in range(nbatch):
            v = kv_ref[g, pl.ds(chunk_start(c), KEY_CHUNK), :][:, A_HEAD_DIM:]
            m_prev = m_s[g][0:1, :]
            m_new = jnp.maximum(m_prev, cmax_s[g, slot][0:1, :])
            alpha = jnp.exp2(m_prev - m_new)
            sums = []
            for h in range(A_HEADS):
                cs = slice(h * Q_BLOCK, (h + 1) * Q_BLOCK)
                p = jnp.exp2(s_s[g, slot, h] - m_new[:, cs])
                p_s[g, slot, :, cs] = p.astype(BF16)
                sums.append(jnp.sum(p, axis=0, keepdims=True))
            l_s[g] = jnp.broadcast_to(alpha * l_s[g][0:1, :] + jnp.concatenate(sums, axis=1), l_s.shape[1:])
            m_s[g] = jnp.broadcast_to(m_new, m_s.shape[1:])
            acc_s[g] = acc_s[g] * alpha + _dot_tn(v, p_s[g, slot])

    def attn_body(j, carry):
        qk_into(2 * j + 1, 1)
        softmax_from(2 * j, 0)
        qk_into(2 * j + 2, 0)
        softmax_from(2 * j + 1, 1)
        return carry

    qk_into(0, 0)
    lax.fori_loop(0, npair, attn_body, 0)

    for g in range(nbatch):
        out_t = acc_s[g] / l_s[g][0:1, :]
        outs = [out_t[:, h * Q_BLOCK:(h + 1) * Q_BLOCK].T for h in range(A_HEADS)]
        o_ref[g] = jnp.concatenate(outs, axis=1).astype(o_ref.dtype)


def _dsa(proj, *, nbatch):
    b, s, _ = proj.shape
    topk = min(TOPK_MAX, s // 4)
    assert topk % Q_BLOCK == 0 and s % KEY_CHUNK == 0 and b % nbatch == 0
    nb = s // Q_BLOCK
    nch_max = s // KEY_CHUNK
    width = A_HEADS * A_HEAD_DIM
    return pl.pallas_call(
        functools.partial(_dsa_kernel, nbatch=nbatch, topk=topk),
        grid=(b // nbatch, nb),
        in_specs=[pl.BlockSpec((nbatch, Q_BLOCK, width), lambda bi, i: (bi, i, EV_AQ)),
                  pl.BlockSpec((nbatch, Q_BLOCK, width), lambda bi, i: (bi, i, EV_IQ)),
                  pl.BlockSpec((nbatch, s, LANES), lambda bi, i: (bi, 0, EV_KV), pipeline_mode=pl.Buffered(1)),
                  pl.BlockSpec((nbatch, s, LANES), lambda bi, i: (bi, 0, EV_KIW), pipeline_mode=pl.Buffered(1))],
        out_specs=pl.BlockSpec((nbatch, Q_BLOCK, width), lambda bi, i: (bi, i, 0)),
        out_shape=jax.ShapeDtypeStruct((b, s, width), BF16),
        scratch_shapes=[
            pltpu.VMEM((nbatch, IDX_DIM, IDX_HEADS * Q_BLOCK), BF16),
            pltpu.VMEM((nbatch, A_HEAD_DIM, A_HEADS * Q_BLOCK), BF16),
            pltpu.VMEM((nbatch, nch_max, KEY_CHUNK, Q_BLOCK), I32),
            pltpu.VMEM((nbatch, nch_max, KEY_CHUNK, Q_BLOCK), jnp.int16),
            pltpu.VMEM((nbatch, SUBLANES, Q_BLOCK), I32),
            pltpu.VMEM((nbatch, 2, KEY_CHUNK, Q_BLOCK), F32),
            pltpu.VMEM((nbatch, 2, A_HEADS, KEY_CHUNK, Q_BLOCK), F32),
            pltpu.VMEM((nbatch, 2, SUBLANES, A_HEADS * Q_BLOCK), F32),
            pltpu.VMEM((nbatch, 2, KEY_CHUNK, A_HEADS * Q_BLOCK), BF16),
            pltpu.VMEM((nbatch, SUBLANES, A_HEADS * Q_BLOCK), F32),
            pltpu.VMEM((nbatch, SUBLANES, A_HEADS * Q_BLOCK), F32),
            pltpu.VMEM((nbatch, A_HEAD_DIM, A_HEADS * Q_BLOCK), F32),
        ],
        compiler_params=_cparams(("parallel", "arbitrary")),
        name="dsa",
    )(proj, proj, proj, proj)


def _gla_constants(chunk):
    t = np.arange(chunk)[:, None]
    r = np.arange(chunk)[None, :]
    blocks = [(r <= t), (r > t)]
    masks = [(t == r)]
    n = chunk // 2
    while n >= 1:
        upper = (t // n) % 2 == 1
        mid = (t // n) * n
        end = (t // n + 1) * n - 1
        blocks.append((upper & (r >= mid) & (r <= t)) | (~upper & (r >= t + 1) & (r <= end)))
        s = r
        masks.append(upper & ((s // n) % 2 == 0) & (t // (2 * n) == s // (2 * n)))
        n //= 2
    return (np.concatenate(blocks, axis=0).astype(np.float32),
            np.stack(masks, axis=0).astype(np.float32))


def _gla_kernel(q_ref, k_ref, v_ref, r_ref, lr_ref, w2_ref, b2_ref, gn_ref, sums_ref, masks_ref,
                o_ref, state_s, e_s, *, chunk):
    nlev = masks_ref.shape[0] - 1

    @pl.when(pl.program_id(1) == 0)
    def _init():
        state_s[...] = jnp.zeros_like(state_s)

    x = _dot(lr_ref[0][:, :G_RANK], w2_ref[...]) + b2_ref[...]
    la = _log_sigmoid(x) * (1.0 / G_TAU)
    hi = la.astype(BF16)
    lo = (la - hi.astype(F32)).astype(BF16)
    dec = _dot(sums_ref[...], jnp.concatenate([hi, lo], axis=0))
    e_s[...] = jnp.exp(dec)

    for h in range(G_HEADS):
        cs = slice(h * G_DK, (h + 1) * G_DK)
        qf = q_ref[0][:, cs].astype(F32) * G_DK ** -0.5
        kb = k_ref[0][:, cs]
        kf = kb.astype(F32)
        vb = v_ref[0][:, cs]
        st = state_s[h]

        o = _dot_nt((qf * e_s[0:chunk, cs]).astype(BF16), st.astype(BF16))
        att = masks_ref[0] * _dot_nt(qf.astype(BF16), kb)
        for lv in range(nlev):
            e_lv = e_s[(2 + lv) * chunk:(3 + lv) * chunk, cs]
            att = att + masks_ref[lv + 1] * _dot_nt((qf * e_lv).astype(BF16), (kf * e_lv).astype(BF16))
        o = o + _dot(att.astype(BF16), vb)

        kd = (kf * e_s[chunk:2 * chunk, cs]).astype(BF16)
        state_s[h] = st * e_s[chunk - 1:chunk, cs] + _dot_tn(vb, kd)

        mu = jnp.mean(o, axis=-1, keepdims=True)
        oc = o - mu
        var = jnp.mean(oc * oc, axis=-1, keepdims=True)
        on = oc * lax.rsqrt(var + LN_EPS) * gn_ref[h:h + 1, :]
        rg = r_ref[0][:, cs].astype(F32)
        o_ref[0, :, cs] = (on * (rg * jax.nn.sigmoid(rg))).astype(o_ref.dtype)


def _gla(proj, w2, b2, gnorm, *, chunk=GLA_CHUNK):
    b, s, _ = proj.shape
    width = G_HEADS * G_DK
    sums, masks = _gla_constants(chunk)
    sums = jnp.asarray(np.concatenate([sums, sums], axis=1), BF16)
    masks = jnp.asarray(masks, F32)
    col = lambda blk: pl.BlockSpec((1, chunk, width), lambda bi, c: (bi, c, blk))
    const2 = lambda bi, c: (0, 0)
    return pl.pallas_call(
        functools.partial(_gla_kernel, chunk=chunk),
        grid=(b, s // chunk),
        in_specs=[col(EV_GQ), col(EV_GK), col(EV_GV), col(EV_GR),
                  pl.BlockSpec((1, chunk, LANES), lambda bi, c: (bi, c, EV_GLR)),
                  pl.BlockSpec(w2.shape, const2),
                  pl.BlockSpec(b2.shape, const2),
                  pl.BlockSpec(gnorm.shape, const2),
                  pl.BlockSpec(sums.shape, const2),
                  pl.BlockSpec(masks.shape, lambda bi, c: (0, 0, 0))],
        out_specs=pl.BlockSpec((1, chunk, width), lambda bi, c: (bi, c, 0)),
        out_shape=jax.ShapeDtypeStruct((b, s, width), BF16),
        scratch_shapes=[pltpu.VMEM((G_HEADS, G_DV, G_DK), F32),
                        pltpu.VMEM((sums.shape[0], width), F32)],
        compiler_params=_cparams(("parallel", "arbitrary")),
        name="gla",
    )(proj, proj, proj, proj, proj, w2, b2, gnorm, sums, masks)


def _mlstm_pre_kernel(x_ref, win_ref, cw_ref, cb_ref, wq_ref, wk_ref, wv_ref, wg_ref, bg_ref,
                      q_ref, k_ref, v_ref, xc_ref, op_ref, gate_ref, ext_s, *, tm):
    @pl.when(pl.program_id(1) == 0)
    def _init():
        ext_s[0:SUBLANES, :] = jnp.zeros((SUBLANES, M_INNER), F32)

    xb = x_ref[0]
    ext_s[SUBLANES:, :] = _dot(xb, win_ref[:, :M_INNER])
    op_ref[0] = _dot(xb, win_ref[:, M_INNER:]).astype(op_ref.dtype)
    gates = bg_ref[...]
    for gi in range(M_INNER // MXU_DIM):
        cs = slice(gi * MXU_DIM, (gi + 1) * MXU_DIM)
        conv = cb_ref[:, cs]
        for j in range(M_CONV):
            off = SUBLANES - (M_CONV - 1) + j
            conv = conv + cw_ref[j:j + 1, cs] * ext_s[off:off + tm, cs]
        x_c = conv * jax.nn.sigmoid(conv)
        xcb = x_c.astype(BF16)
        xc_ref[0, :, cs] = xcb
        xmb = ext_s[SUBLANES:, cs].astype(BF16)
        for src, w_ref, dst, part in ((xcb, wq_ref, q_ref, 0), (xcb, wk_ref, k_ref, 1), (xmb, wv_ref, v_ref, 2)):
            y = _dot(src, w_ref[gi]).astype(BF16)
            dst[0, :, cs] = y
            gates = gates + _dot(y, wg_ref[part * M_INNER + gi * MXU_DIM:part * M_INNER + (gi + 1) * MXU_DIM, :])
    ext_s[0:SUBLANES, :] = ext_s[tm:tm + SUBLANES, :]
    gate_ref[0] = gates[:, :2 * M_HEADS]


def _mlstm_pre(xb, win, cw, cb, wq, wk, wv, wg, bg, *, tm):
    b, s, d = xb.shape
    const2 = lambda bi, t: (0, 0)
    const3 = lambda bi, t: (0, 0, 0)
    tile = lambda dt: jax.ShapeDtypeStruct((b, s, M_INNER), dt)
    ospec = pl.BlockSpec((1, tm, M_INNER), lambda bi, t: (bi, t, 0))
    return pl.pallas_call(
        functools.partial(_mlstm_pre_kernel, tm=tm),
        grid=(b, s // tm),
        in_specs=[pl.BlockSpec((1, tm, d), lambda bi, t: (bi, t, 0)),
                  pl.BlockSpec(win.shape, const2, pipeline_mode=pl.Buffered(1)),
                  pl.BlockSpec(cw.shape, const2),
                  pl.BlockSpec(cb.shape, const2),
                  pl.BlockSpec(wq.shape, const3),
                  pl.BlockSpec(wk.shape, const3),
                  pl.BlockSpec(wv.shape, const3),
                  pl.BlockSpec(wg.shape, const2),
                  pl.BlockSpec(bg.shape, const2)],
        out_specs=[ospec, ospec, ospec, ospec, ospec,
                   pl.BlockSpec((1, tm, 2 * M_HEADS), lambda bi, t: (bi, t, 0))],
        out_shape=[tile(BF16), tile(BF16), tile(BF16), tile(BF16), tile(BF16),
                   jax.ShapeDtypeStruct((b, s, 2 * M_HEADS), F32)],
        scratch_shapes=[pltpu.VMEM((tm + SUBLANES, M_INNER), F32)],
        compiler_params=_cparams(("parallel", "arbitrary")),
        name="mlstm_pre",
    )(xb, win, cw, cb, wq, wk, wv, wg, bg)


def _mlstm_kernel(q_ref, k_ref, v_ref, xc_ref, op_ref, gc_ref, gr_ref, skip_ref, ng_ref, tri_ref, ones_ref,
                  y_ref, c_s, n_s, m_s, num_s, rden_s, *, chunk):
    @pl.when(pl.program_id(1) == 0)
    def _init():
        c_s[...] = jnp.zeros_like(c_s)
        n_s[...] = jnp.zeros_like(n_s)
        m_s[...] = jnp.zeros_like(m_s)

    tri = tri_ref[...]
    t_idx = lax.broadcasted_iota(I32, (chunk, chunk), 0)
    s_idx = lax.broadcasted_iota(I32, (chunk, chunk), 1)
    causal = s_idx <= t_idx
    gc = gc_ref[0]
    gr = gr_ref[0]
    for h in range(M_HEADS):
        cs = slice(h * M_DH, (h + 1) * M_DH)
        li_c = gc[:, h:h + 1]
        lf_c = _log_sigmoid(gc[:, M_HEADS + h:M_HEADS + h + 1])
        li_r = gr[h:h + 1, :]
        lf_r = _log_sigmoid(gr[M_HEADS + h:M_HEADS + h + 1, :])
        bc2 = _dot(tri, _split_bf16(jnp.broadcast_to(lf_c, (chunk, LANES))))
        b_c = (bc2[:, :LANES] + bc2[:, LANES:])[:, 0:1]
        lf_r8 = jnp.broadcast_to(lf_r, (SUBLANES, chunk))
        hi = lf_r8.astype(BF16)
        lo = (lf_r8 - hi.astype(F32)).astype(BF16)
        br2 = _dot_nt(jnp.concatenate([hi, lo], axis=0), tri)
        b_r = br2[0:1, :] + br2[SUBLANES:SUBLANES + 1, :]

        m_prev = m_s[h][0:1, 0:1]
        dmat = jnp.where(causal, b_c - b_r + li_r, -jnp.inf)
        inter = b_c + m_prev
        m_t = jnp.maximum(inter, jnp.max(dmat, axis=1, keepdims=True))
        w_inter = jnp.exp(inter - m_t)

        qb = q_ref[0, :, cs]
        kb = (k_ref[0, :, cs].astype(F32) * M_DH ** -0.5).astype(BF16)
        vb = v_ref[0, :, cs]
        n_prev = n_s[h][0:1, :]

        pb = (jnp.exp(dmat - m_t) * _dot_nt(qb, kb)).astype(BF16)
        num_s[...] = w_inter * _dot(qb, c_s[h].astype(BF16)) + _dot(pb, vb)
        qn = _dot_nt(qb, n_s[h].astype(BF16))[:, 0:1]
        den = w_inter * qn + _dot(pb, ones_ref[...])[:, 0:1]
        rden_s[...] = jnp.broadcast_to(1.0 / jnp.maximum(jnp.abs(den), jnp.exp(-m_t)), rden_s.shape)

        g_tot = b_c[chunk - 1:chunk, :]
        d_end = g_tot - b_c + li_c
        m_new = jnp.maximum(g_tot + m_prev, jnp.max(d_end, axis=0, keepdims=True))
        wk = jnp.exp(d_end - m_new)
        decay = jnp.exp(g_tot + m_prev - m_new)
        kw = k_ref[0, :, cs].astype(F32) * (wk * M_DH ** -0.5)
        kwb = kw.astype(BF16)
        n_s[h] = jnp.broadcast_to(decay * n_prev + jnp.sum(kw, axis=0, keepdims=True), n_s.shape[1:])
        m_s[h] = jnp.broadcast_to(m_new, m_s.shape[1:])
        for r in range(M_DH // LANES):
            rs = slice(r * LANES, (r + 1) * LANES)
            c_s[h, rs, :] = decay * c_s[h, rs, :] + _dot_tn(kwb[:, rs], vb)

        slab = min(chunk, 64)
        for r in range(chunk // slab):
            rs = slice(r * slab, (r + 1) * slab)
            hg = num_s[rs, :] * rden_s[rs, 0:1] * jax.nn.sigmoid(op_ref[0, rs, cs].astype(F32))
            mu = jnp.mean(hg, axis=-1, keepdims=True)
            hc = hg - mu
            var = jnp.mean(hc * hc, axis=-1, keepdims=True)
            hn = hc * lax.rsqrt(var + LN_EPS) * ng_ref[h:h + 1, :]
            y_ref[0, rs, cs] = (hn + skip_ref[:, cs] * xc_ref[0, rs, cs].astype(F32)).astype(y_ref.dtype)


def _mlstm(q, k, v, xc, op, gates, skip, norm_g, *, chunk=MLSTM_CHUNK):
    b, s, _ = q.shape
    gates_t = jnp.swapaxes(gates, 1, 2)
    tri = jnp.asarray(np.tril(np.ones((chunk, chunk), np.float32)), BF16)
    ones = jnp.ones((chunk, LANES), BF16)
    big = pl.BlockSpec((1, chunk, M_INNER), lambda bi, c: (bi, c, 0))
    const2 = lambda bi, c: (0, 0)
    return pl.pallas_call(
        functools.partial(_mlstm_kernel, chunk=chunk),
        grid=(b, s // chunk),
        in_specs=[big, big, big, big, big,
                  pl.BlockSpec((1, chunk, 2 * M_HEADS), lambda bi, c: (bi, c, 0)),
                  pl.BlockSpec((1, 2 * M_HEADS, chunk), lambda bi, c: (bi, 0, c)),
                  pl.BlockSpec(skip.shape, const2),
                  pl.BlockSpec(norm_g.shape, const2),
                  pl.BlockSpec(tri.shape, const2),
                  pl.BlockSpec(ones.shape, const2)],
        out_specs=big,
        out_shape=jax.ShapeDtypeStruct((b, s, M_INNER), BF16),
        scratch_shapes=[pltpu.VMEM((M_HEADS, M_DH, M_DH), F32),
                        pltpu.VMEM((M_HEADS, SUBLANES, M_DH), F32),
                        pltpu.VMEM((M_HEADS, SUBLANES, LANES), F32),
                        pltpu.VMEM((chunk, M_DH), F32),
                        pltpu.VMEM((chunk, LANES), F32)],
        compiler_params=_cparams(("parallel", "arbitrary")),
        name="mlstm",
    )(q, k, v, xc, op, gates, gates_t, skip, norm_g, tri, ones)


def _even_w_in(w):
    sizes = (512, 64, 64, 512, 64, 8, 512, 512, 512, 512, 16)
    offs = np.concatenate([[0], np.cumsum(sizes)])
    part = lambda j: w[:, offs[j]:offs[j + 1]]
    a_q, a_k, a_v, i_q, i_k, i_w, g_q, g_k, g_v, g_r, g_lr = (part(j) for j in range(11))
    zeros = lambda n: jnp.zeros((w.shape[0], n), w.dtype)
    cols = [a_q, i_q, g_q, g_k, g_v, g_r, a_k, a_v, i_k, i_w, zeros(LANES - 72), g_lr, zeros(LANES - G_RANK)]
    return jnp.concatenate(cols, axis=1).astype(BF16)


def _block_diag(w):
    per = MXU_DIM // M_QKV_BLOCK
    wg = w.reshape(-1, per, M_QKV_BLOCK, M_QKV_BLOCK)
    eye = jnp.eye(per, dtype=w.dtype)
    dense = jnp.einsum('gade,ab->gadbe', wg, eye)
    return dense.reshape(-1, MXU_DIM, MXU_DIM).astype(BF16)


def _token_tile(t):
    for tm in (512, 256, 128):
        if t % tm == 0:
            return tm
    raise ValueError(f"token count {t} is not a multiple of 128")


def kernel(x, ev_w_in, ev_g_w2, ev_g_b2, ev_g_norm, ev_w_out, od_w_in, od_conv_w, od_conv_b, od_w_q, od_w_k, od_w_v, od_w_gate, od_b_gate, od_skip, od_norm, od_w_out, ln1_g, ln1_b, ln2_g, ln2_b, mlp_w1, mlp_w2):
    b, s, d = x.shape
    t = b * s
    tm = _token_tile(t)
    xf = x.reshape(t, d)
    xb = xf.astype(BF16)
    row = lambda v: v.reshape(1, -1)
    a_width = A_HEADS * A_HEAD_DIM
    for l in range(DEPTH):
        j = l // 2
        if l % 2 == 0:
            proj = _matmul(xb, _even_w_in(ev_w_in[j]), tm=tm, tn=EV_COLS_PADDED // 3, out_dtype=BF16)
            proj = proj.reshape(b, s, EV_COLS_PADDED)
            a_out = _dsa(proj, nbatch=next(n for n in (4, 2, 1) if b % n == 0))
            g_out = _gla(proj, ev_g_w2[j].astype(BF16), row(ev_g_b2[j]), ev_g_norm[j])
            w_out = ev_w_out[j].astype(BF16)
            acts = [a_out.reshape(t, a_width), g_out.reshape(t, -1)]
            weights = [w_out[:a_width], w_out[a_width:]]
        else:
            wg = jnp.pad(od_w_gate[j], ((0, 0), (0, LANES - 2 * M_HEADS))).astype(BF16)
            bg = jnp.pad(od_b_gate[j], (0, LANES - 2 * M_HEADS)).reshape(1, LANES)
            q, k, v, xc, op, gates = _mlstm_pre(
                xb.reshape(b, s, d), od_w_in[j].astype(BF16), od_conv_w[j], row(od_conv_b[j]),
                _block_diag(od_w_q[j]), _block_diag(od_w_k[j]), _block_diag(od_w_v[j]), wg, bg,
                tm=min(256, s))
            y = _mlstm(q, k, v, xc, op, gates, row(od_skip[j]), od_norm[j], chunk=min(MLSTM_CHUNK, s))
            acts = [y.reshape(t, M_INNER)]
            weights = [od_w_out[j].astype(BF16)]
        xf, xb = _layer_tail(acts, weights, xf, row(ln1_g[l]), row(ln1_b[l]),
                             mlp_w1[l].astype(BF16), mlp_w2[l].astype(BF16), row(ln2_g[l]), row(ln2_b[l]), tm=tm)
    return xf.reshape(b, s, d)
```

```python
import functools

import numpy as np
import jax
import jax.numpy as jnp
from jax import lax
from jax.experimental import pallas as pl
from jax.experimental.pallas import tpu as pltpu

F32 = jnp.float32
BF16 = jnp.bfloat16
I32 = jnp.int32

D_MODEL = 1024
DEPTH = 4
ALPHA = (2 * DEPTH) ** 0.25
LN_EPS = 1e-5
A_HEADS = 8
A_HEAD_DIM = 64
IDX_HEADS = 8
IDX_DIM = 64
TOPK_MAX = 256
Q_BLOCK = 128
KEY_CHUNK = 256
G_HEADS = 4
G_DK = 128
G_DV = 128
G_RANK = 16
G_TAU = 16.0
GLA_CHUNK = 128
M_INNER = 2 * D_MODEL
M_HEADS = 4
M_DH = M_INNER // M_HEADS
M_CONV = 4
M_QKV_BLOCK = 4
MLSTM_CHUNK = 256
D_FF = 4 * D_MODEL

LANES = 128
SUBLANES = 8
MXU_DIM = 256
VMEM_LIMIT = 56 * 1024 * 1024

EV_AQ, EV_IQ, EV_GQ, EV_GK, EV_GV, EV_GR = 0, 1, 2, 3, 4, 5
EV_KV, EV_KIW, EV_GLR = 24, 25, 26
EV_COLS_PADDED = 27 * LANES

INT_MIN = -2147483648
KEY_NEG_INF = INT_MIN - (-8388608)
MASK_BIAS = -2e30
M_INIT = -1e30
LOG2_E = 1.4426950408889634


def _cparams(sem):
    return pltpu.CompilerParams(dimension_semantics=sem, vmem_limit_bytes=VMEM_LIMIT)


def _layer_norm(z, g, b):
    mu = jnp.mean(z, axis=-1, keepdims=True)
    zc = z - mu
    var = jnp.mean(zc * zc, axis=-1, keepdims=True)
    return zc * lax.rsqrt(var + LN_EPS) * g + b


def _dot(a, b):
    return jnp.dot(a, b, preferred_element_type=F32)


def _dot_nt(a, b):
    return lax.dot_general(a, b, (((1,), (1,)), ((), ())), preferred_element_type=F32)


def _dot_tn(a, b):
    return lax.dot_general(a, b, (((0,), (0,)), ((), ())), preferred_element_type=F32)


def _split_bf16(x):
    hi = x.astype(BF16)
    lo = (x - hi.astype(F32)).astype(BF16)
    return jnp.concatenate([hi, lo], axis=-1)


def _log_sigmoid(x):
    return jnp.minimum(x, 0.0) - jnp.log1p(jnp.exp(-jnp.abs(x)))


def _matmul_kernel(x_ref, w_ref, o_ref, *, tn):
    x = x_ref[...]
    for j in range(w_ref.shape[1] // tn):
        o_ref[:, j * tn:(j + 1) * tn] = _dot(x, w_ref[:, j * tn:(j + 1) * tn]).astype(o_ref.dtype)


def _matmul(x, w, *, tm, tn, out_dtype):
    t, k = x.shape
    n = w.shape[1]
    assert n % tn == 0
    return pl.pallas_call(
        functools.partial(_matmul_kernel, tn=tn),
        grid=(t // tm,),
        in_specs=[pl.BlockSpec((tm, k), lambda i: (i, 0)),
                  pl.BlockSpec((k, n), lambda i: (0, 0), pipeline_mode=pl.Buffered(1))],
        out_specs=pl.BlockSpec((tm, n), lambda i: (i, 0)),
        out_shape=jax.ShapeDtypeStruct((t, n), out_dtype),
        compiler_params=_cparams(("parallel",)),
        name="in_proj",
    )(x, w)


def _layer_tail_kernel(*refs, n_in, ff_chunk):
    a_refs, wo_refs = refs[:n_in], refs[n_in:2 * n_in]
    xf_ref, g1_ref, b1_ref, w1_ref, w2_ref, g2_ref, b2_ref, of_ref, ob_ref = refs[2 * n_in:]
    acc = _dot(a_refs[0][...], wo_refs[0][...])
    for a_ref, w_ref in zip(a_refs[1:], wo_refs[1:]):
        acc = acc + _dot(a_ref[...], w_ref[...])
    x1 = _layer_norm(ALPHA * xf_ref[...] + acc, g1_ref[...], b1_ref[...])
    xb = x1.astype(BF16)
    acc = None
    for c in range(D_FF // ff_chunk):
        h = _dot(xb, w1_ref[:, c * ff_chunk:(c + 1) * ff_chunk])
        h = jnp.maximum(h, 0.0)
        part = _dot((h * h).astype(BF16), w2_ref[c * ff_chunk:(c + 1) * ff_chunk, :])
        acc = part if acc is None else acc + part
    y = _layer_norm(ALPHA * x1 + acc, g2_ref[...], b2_ref[...])
    of_ref[...] = y
    ob_ref[...] = y.astype(BF16)


def _layer_tail(acts, wos, xf, g1, b1, w1, w2, g2, b2, *, tm, ff_chunk=1024):
    t, d = xf.shape
    const = lambda i: (0, 0)
    resident = lambda w: pl.BlockSpec(w.shape, const, pipeline_mode=pl.Buffered(1))
    vec = pl.BlockSpec((1, d), const)
    tile = pl.BlockSpec((tm, d), lambda i: (i, 0))
    in_specs = [pl.BlockSpec((tm, a.shape[1]), lambda i: (i, 0)) for a in acts]
    in_specs += [resident(w) for w in wos]
    in_specs += [tile, vec, vec, resident(w1), resident(w2), vec, vec]
    return pl.pallas_call(
        functools.partial(_layer_tail_kernel, n_in=len(acts), ff_chunk=ff_chunk),
        grid=(t // tm,),
        in_specs=in_specs,
        out_specs=[tile, tile],
        out_shape=[jax.ShapeDtypeStruct((t, d), F32), jax.ShapeDtypeStruct((t, d), BF16)],
        compiler_params=_cparams(("parallel",)),
        name="proj_ln_mlp_ln",
    )(*acts, *wos, xf, g1, b1, w1, w2, g2, b2)


def _sortable_key(x):
    bits = pltpu.bitcast(x, I32)
    return jnp.where(bits < 0, INT_MIN - bits, bits)


def _column_count(hit):
    quarter = jnp.sum(hit.reshape(4, KEY_CHUNK // 4, LANES), axis=0)
    return jnp.sum(quarter.reshape(KEY_CHUNK // 4 // SUBLANES, SUBLANES, LANES), axis=0)


def _dsa_kernel(q_ref, qi_ref, kv_ref, kiw_ref, o_ref,
                qi_s, qa_s, skey_s, skey16_s, thr_s, bias_s, s_s, cmax_s, p_s, m_s, l_s, acc_s, *, nbatch, topk):
    i = pl.program_id(1)
    nch = i // 2 + 1
    key_in_chunk = lax.broadcasted_iota(I32, (KEY_CHUNK, Q_BLOCK), 0)
    q_in_block = lax.broadcasted_iota(I32, (KEY_CHUNK, Q_BLOCK), 1)
    nch_max = skey_s.shape[1]

    sel_r = lax.broadcasted_iota(I32, (2 * SUBLANES, LANES), 0)
    sel_c = lax.broadcasted_iota(I32, (2 * SUBLANES, LANES), 1)
    sel = jnp.where(sel_c == sel_r + IDX_DIM, 1.0, 0.0).astype(BF16)
    w_rows = []
    for g in range(nbatch):
        qi = qi_ref[g].astype(F32)
        q = q_ref[g].astype(F32) * (A_HEAD_DIM ** -0.5 * LOG2_E)
        for h in range(IDX_HEADS):
            qi_s[g, :, h * Q_BLOCK:(h + 1) * Q_BLOCK] = qi[:, h * IDX_DIM:(h + 1) * IDX_DIM].T.astype(BF16)
        for h in range(A_HEADS):
            qa_s[g, :, h * Q_BLOCK:(h + 1) * Q_BLOCK] = q[:, h * A_HEAD_DIM:(h + 1) * A_HEAD_DIM].T.astype(BF16)
        w_blk = kiw_ref[g, pl.ds(pl.multiple_of(i * Q_BLOCK, Q_BLOCK), Q_BLOCK), :]
        w_rows.append(_dot_nt(sel, w_blk)[:IDX_HEADS] * (IDX_HEADS ** -0.5 * IDX_DIM ** -0.5))
        m_s[g] = jnp.full(m_s.shape[1:], M_INIT, F32)
        l_s[g] = jnp.zeros(l_s.shape[1:], F32)
        acc_s[g] = jnp.zeros(acc_s.shape[1:], F32)

    npair = (nch + 1) // 2

    def chunk_start(c):
        return pl.multiple_of(jnp.minimum(c, nch - 1) * KEY_CHUNK, KEY_CHUNK)

    def store_heads(g, slot, res):
        for h in range(A_HEADS):
            s_s[g, slot, h] = res[:, h * Q_BLOCK:(h + 1) * Q_BLOCK]

    def logits_into(c, slot):
        for g in range(nbatch):
            ki = kiw_ref[g, pl.ds(chunk_start(c), KEY_CHUNK), :][:, :IDX_DIM]
            store_heads(g, slot, _dot(ki, qi_s[g]))

    def score_from(c, slot):
        valid = (chunk_start(c) + key_in_chunk) <= (i * Q_BLOCK + q_in_block)
        for g in range(nbatch):
            sc = None
            for h in range(IDX_HEADS):
                t = w_rows[g][h:h + 1, :] * jnp.maximum(s_s[g, slot, h], 0.0)
                sc = t if sc is None else sc + t
            key = _sortable_key(jnp.where(valid, sc, -jnp.inf))
            skey_s[g, jnp.minimum(c, nch - 1)] = key
            skey16_s[g, jnp.minimum(c, nch - 1)] = lax.shift_right_arithmetic(key, 16).astype(jnp.int16)

    def score_body(j, carry):
        logits_into(2 * j + 1, 1)
        score_from(2 * j, 0)
        logits_into(2 * j + 2, 0)
        score_from(2 * j + 1, 1)
        return carry

    logits_into(0, 0)
    lax.fori_loop(0, npair, score_body, 0)

    def count_where(pred):
        def body(c, accs):
            return tuple(accs[g] + _column_count(jnp.where(pred(g, c, skey_s[g, c]), 1, 0))
                         for g in range(nbatch))
        zero = tuple(jnp.zeros((SUBLANES, LANES), I32) for _ in range(nbatch))
        accs = lax.fori_loop(0, nch, body, zero)
        return [jnp.sum(a, axis=0, keepdims=True) for a in accs]

    for g in range(nbatch):
        thr_s[g] = jnp.full(thr_s.shape[1:], KEY_NEG_INF + 1, I32)

    def count_high_ge(cands):
        rows = 2 * SUBLANES
        cands16 = [jnp.broadcast_to(cd, (rows, LANES)).astype(jnp.int16) for cd in cands]

        def body(c, accs):
            out = []
            for g in range(nbatch):
                hi = skey16_s[g, c]
                parts = [jnp.where(hi[r * rows:(r + 1) * rows] >= cands16[g], jnp.bfloat16(1), jnp.bfloat16(0))
                         for r in range(KEY_CHUNK // rows)]
                while len(parts) > 1:
                    parts = [parts[a] + parts[a + 1] for a in range(0, len(parts), 2)]
                out.append(accs[g] + parts[0])
            return tuple(out)

        zero = tuple(jnp.zeros((rows, LANES), BF16) for _ in range(nbatch))
        accs = lax.fori_loop(0, nch, body, zero)
        return [jnp.sum(a.astype(F32), axis=0, keepdims=True) for a in accs]

    @pl.when(i * Q_BLOCK >= topk)
    def _search():
        def high_body(it, los):
            inc = lax.shift_left(jnp.int32(1), 15 - it)
            cands = [lo + inc for lo in los]
            cnt = count_high_ge(cands)
            return tuple(jnp.where(cnt[g] >= topk, cands[g], los[g]) for g in range(nbatch))

        highs = lax.fori_loop(0, 16, high_body,
                              tuple(jnp.full((1, LANES), -32768, I32) for _ in range(nbatch)))

        above = count_high_ge([hi + 1 for hi in highs])
        need_low = [topk - above[g] for g in range(nbatch)]
        rows = 2 * SUBLANES
        h16 = [jnp.broadcast_to(hi, (rows, LANES)).astype(jnp.int16) for hi in highs]

        def bucket_body(c, carry):
            for g in range(nbatch):
                low = (lax.bitwise_and(skey_s[g, c], 0xFFFF) - 32768).astype(jnp.int16)
                for r in range(KEY_CHUNK // rows):
                    rs = slice(r * rows, (r + 1) * rows)
                    skey16_s[g, c, rs, :] = jnp.where(skey16_s[g, c, rs, :] == h16[g], low[rs], jnp.int16(-32768))
            return carry

        lax.fori_loop(0, nch, bucket_body, 0)

        def low_body(it, los):
            inc = lax.shift_left(jnp.int32(1), 15 - it)
            cands = [lo + inc for lo in los]
            cnt = count_high_ge(cands)
            return tuple(jnp.where(cnt[g] >= need_low[g], cands[g], los[g]) for g in range(nbatch))

        lows = lax.fori_loop(0, 16, low_body,
                             tuple(jnp.full((1, LANES), -32768, I32) for _ in range(nbatch)))
        los = [lax.shift_left(highs[g], 16) + (lows[g] + 32768) for g in range(nbatch)]
        for g in range(nbatch):
            thr_s[g] = jnp.broadcast_to(los[g], thr_s.shape[1:])

        cnt_ge = count_where(lambda g, c, kk: kk >= los[g])
        worst = cnt_ge[0]
        for g in range(1, nbatch):
            worst = jnp.maximum(worst, cnt_ge[g])

        @pl.when(jnp.max(worst) > topk)
        def _ties():
            cnt_gt = count_where(lambda g, c, kk: kk > los[g])
            need = [topk - cnt_gt[g] for g in range(nbatch)]
            nbits = max(1, int(np.ceil(np.log2(nch_max * KEY_CHUNK))))

            def idx_body(it, ps):
                inc = lax.shift_left(jnp.int32(1), nbits - 1 - it)

                def pred(g, c, kk):
                    idx = c * KEY_CHUNK + key_in_chunk
                    return jnp.logical_and(kk == los[g], idx < ps[g] + inc)

                cnt = count_where(pred)
                return tuple(jnp.where(cnt[g] < need[g], ps[g] + inc, ps[g]) for g in range(nbatch))

            ps = lax.fori_loop(0, nbits, idx_body,
                               tuple(jnp.zeros((1, LANES), I32) for _ in range(nbatch)))

            def demote_body(c, carry):
                idx = c * KEY_CHUNK + key_in_chunk
                for g in range(nbatch):
                    kk = skey_s[g, c]
                    drop = jnp.logical_and(kk == los[g], idx > ps[g])
                    skey_s[g, c] = jnp.where(drop, INT_MIN, kk)
                return carry

            lax.fori_loop(0, nch, demote_body, 0)

    def qk_into(c, slot):
        cc = jnp.minimum(c, nch - 1)
        for g in range(nbatch):
            k = kv_ref[g, pl.ds(chunk_start(c), KEY_CHUNK), :][:, :A_HEAD_DIM]
            thr = jnp.where(c < nch, thr_s[g][0:1, :], jnp.int32(2147483647))
            bias_s[g, slot] = jnp.where(skey_s[g, cc] >= thr, 0.0, MASK_BIAS)
            res = _dot(k, qa_s[g])
            cmax = []
            for h in range(A_HEADS):
                sh = res[:, h * Q_BLOCK:(h + 1) * Q_BLOCK] + bias_s[g, slot]
                s_s[g, slot, h] = sh
                cmax.append(jnp.max(sh, axis=0, keepdims=True))
            cmax_s[g, slot] = jnp.broadcast_to(jnp.concatenate(cmax, axis=1), cmax_s.shape[2:])

    def softmax_from(c, slot):
        for g in range(nbatch):
            v = kv_ref[g, pl.ds(chunk_start(c), KEY_CHUNK), :][:, A_HEAD_DIM:]
            m_prev = m_s[g][0:1, :]
            m_new = jnp.maximum(m_prev, cmax_s[g, slot][0:1, :])
            alpha = jnp.exp2(m_prev - m_new)
            sums = []
            for h in range(A_HEADS):
                cs = slice(h * Q_BLOCK, (h + 1) * Q_BLOCK)
                p = jnp.exp2(s_s[g, slot, h] - m_new[:, cs])
                p_s[g, slot, :, cs] = p.astype(BF16)
                sums.append(jnp.sum(p, axis=0, keepdims=True))
            l_s[g] = jnp.broadcast_to(alpha * l_s[g][0:1, :] + jnp.concatenate(sums, axis=1), l_s.shape[1:])
            m_s[g] = jnp.broadcast_to(m_new, m_s.shape[1:])
            acc_s[g] = acc_s[g] * alpha + _dot_tn(v, p_s[g, slot])

    def attn_body(j, carry):
        qk_into(2 * j + 1, 1)
        softmax_from(2 * j, 0)
        qk_into(2 * j + 2, 0)
        softmax_from(2 * j + 1, 1)
        return carry

    qk_into(0, 0)
    lax.fori_loop(0, npair, attn_body, 0)

    for g in range(nbatch):
        out_t = acc_s[g] / l_s[g][0:1, :]
        outs = [out_t[:, h * Q_BLOCK:(h + 1) * Q_BLOCK].T for h in range(A_HEADS)]
        o_ref[g] = jnp.concatenate(outs, axis=1).astype(o_ref.dtype)


def _dsa(proj, *, nbatch):
    b, s, _ = proj.shape
    topk = min(TOPK_MAX, s // 4)
    assert topk % Q_BLOCK == 0 and s % KEY_CHUNK == 0 and b % nbatch == 0
    nb = s // Q_BLOCK
    nch_max = s // KEY_CHUNK
    width = A_HEADS * A_HEAD_DIM
    return pl.pallas_call(
        functools.partial(_dsa_kernel, nbatch=nbatch, topk=topk),
        grid=(b // nbatch, nb),
        in_specs=[pl.BlockSpec((nbatch, Q_BLOCK, width), lambda bi, i: (bi, i, EV_AQ)),
                  pl.BlockSpec((nbatch, Q_BLOCK, width), lambda bi, i: (bi, i, EV_IQ)),
                  pl.BlockSpec((nbatch, s, LANES), lambda bi, i: (bi, 0, EV_KV), pipeline_mode=pl.Buffered(1)),
                  pl.BlockSpec((nbatch, s, LANES), lambda bi, i: (bi, 0, EV_KIW), pipeline_mode=pl.Buffered(1))],
        out_specs=pl.BlockSpec((nbatch, Q_BLOCK, width), lambda bi, i: (bi, i, 0)),
        out_shape=jax.ShapeDtypeStruct((b, s, width), BF16),
        scratch_shapes=[
            pltpu.VMEM((nbatch, IDX_DIM, IDX_HEADS * Q_BLOCK), BF16),
            pltpu.VMEM((nbatch, A_HEAD_DIM, A_HEADS * Q_BLOCK), BF16),
            pltpu.VMEM((nbatch, nch_max, KEY_CHUNK, Q_BLOCK), I32),
            pltpu.VMEM((nbatch, nch_max, KEY_CHUNK, Q_BLOCK), jnp.int16),
            pltpu.VMEM((nbatch, SUBLANES, Q_BLOCK), I32),
            pltpu.VMEM((nbatch, 2, KEY_CHUNK, Q_BLOCK), F32),
            pltpu.VMEM((nbatch, 2, A_HEADS, KEY_CHUNK, Q_BLOCK), F32),
            pltpu.VMEM((nbatch, 2, SUBLANES, A_HEADS * Q_BLOCK), F32),
            pltpu.VMEM((nbatch, 2, KEY_CHUNK, A_HEADS * Q_BLOCK), BF16),
            pltpu.VMEM((nbatch, SUBLANES, A_HEADS * Q_BLOCK), F32),
            pltpu.VMEM((nbatch, SUBLANES, A_HEADS * Q_BLOCK), F32),
            pltpu.VMEM((nbatch, A_HEAD_DIM, A_HEADS * Q_BLOCK), F32),
        ],
        compiler_params=_cparams(("parallel", "arbitrary")),
        name="dsa",
    )(proj, proj, proj, proj)


def _gla_constants(chunk):
    t = np.arange(chunk)[:, None]
    r = np.arange(chunk)[None, :]
    blocks = [(r <= t), (r > t)]
    masks = [(t == r)]
    n = chunk // 2
    while n >= 1:
        upper = (t // n) % 2 == 1
        mid = (t // n) * n
        end = (t // n + 1) * n - 1
        blocks.append((upper & (r >= mid) & (r <= t)) | (~upper & (r >= t + 1) & (r <= end)))
        s = r
        masks.append(upper & ((s // n) % 2 == 0) & (t // (2 * n) == s // (2 * n)))
        n //= 2
    return (np.concatenate(blocks, axis=0).astype(np.float32),
            np.stack(masks, axis=0).astype(np.float32))


def _gla_kernel(q_ref, k_ref, v_ref, r_ref, lr_ref, w2_ref, b2_ref, gn_ref, sums_ref, masks_ref,
                o_ref, state_s, e_s, *, chunk):
    nlev = masks_ref.shape[0] - 1

    @pl.when(pl.program_id(1) == 0)
    def _init():
        state_s[...] = jnp.zeros_like(state_s)

    x = _dot(lr_ref[0][:, :G_RANK], w2_ref[...]) + b2_ref[...]
    la = _log_sigmoid(x) * (1.0 / G_TAU)
    hi = la.astype(BF16)
    lo = (la - hi.astype(F32)).astype(BF16)
    dec = _dot(sums_ref[...], jnp.concatenate([hi, lo], axis=0))
    e_s[...] = jnp.exp(dec)

    for h in range(G_HEADS):
        cs = slice(h * G_DK, (h + 1) * G_DK)
        qf = q_ref[0][:, cs].astype(F32) * G_DK ** -0.5
        kb = k_ref[0][:, cs]
        kf = kb.astype(F32)
        vb = v_ref[0][:, cs]
        st = state_s[h]

        o = _dot_nt((qf * e_s[0:chunk, cs]).astype(BF16), st.astype(BF16))
        att = masks_ref[0] * _dot_nt(qf.astype(BF16), kb)
        for lv in range(nlev):
            e_lv = e_s[(2 + lv) * chunk:(3 + lv) * chunk, cs]
            att = att + masks_ref[lv + 1] * _dot_nt((qf * e_lv).astype(BF16), (kf * e_lv).astype(BF16))
        o = o + _dot(att.astype(BF16), vb)

        kd = (kf * e_s[chunk:2 * chunk, cs]).astype(BF16)
        state_s[h] = st * e_s[chunk - 1:chunk, cs] + _dot_tn(vb, kd)

        mu = jnp.mean(o, axis=-1, keepdims=True)
        oc = o - mu
        var = jnp.mean(oc * oc, axis=-1, keepdims=True)
        on = oc * lax.rsqrt(var + LN_EPS) * gn_ref[h:h + 1, :]
        rg = r_ref[0][:, cs].astype(F32)
        o_ref[0, :, cs] = (on * (rg * jax.nn.sigmoid(rg))).astype(o_ref.dtype)


def _gla(proj, w2, b2, gnorm, *, chunk=GLA_CHUNK):
    b, s, _ = proj.shape
    width = G_HEADS * G_DK
    sums, masks = _gla_constants(chunk)
    sums = jnp.asarray(np.concatenate([sums, sums], axis=1), BF16)
    masks = jnp.asarray(masks, F32)
    col = lambda blk: pl.BlockSpec((1, chunk, width), lambda bi, c: (bi, c, blk))
    const2 = lambda bi, c: (0, 0)
    return pl.pallas_call(
        functools.partial(_gla_kernel, chunk=chunk),
        grid=(b, s // chunk),
        in_specs=[col(EV_GQ), col(EV_GK), col(EV_GV), col(EV_GR),
                  pl.BlockSpec((1, chunk, LANES), lambda bi, c: (bi, c, EV_GLR)),
                  pl.BlockSpec(w2.shape, const2),
                  pl.BlockSpec(b2.shape, const2),
                  pl.BlockSpec(gnorm.shape, const2),
                  pl.BlockSpec(sums.shape, const2),
                  pl.BlockSpec(masks.shape, lambda bi, c: (0, 0, 0))],
        out_specs=pl.BlockSpec((1, chunk, width), lambda bi, c: (bi, c, 0)),
        out_shape=jax.ShapeDtypeStruct((b, s, width), BF16),
        scratch_shapes=[pltpu.VMEM((G_HEADS, G_DV, G_DK), F32),
                        pltpu.VMEM((sums.shape[0], width), F32)],
        compiler_params=_cparams(("parallel", "arbitrary")),
        name="gla",
    )(proj, proj, proj, proj, proj, w2, b2, gnorm, sums, masks)


def _mlstm_pre_kernel(x_ref, win_ref, cw_ref, cb_ref, wq_ref, wk_ref, wv_ref, wg_ref, bg_ref,
                      q_ref, k_ref, v_ref, xc_ref, op_ref, gate_ref, ext_s, *, tm):
    @pl.when(pl.program_id(1) == 0)
    def _init():
        ext_s[0:SUBLANES, :] = jnp.zeros((SUBLANES, M_INNER), F32)

    xb = x_ref[0]
    ext_s[SUBLANES:, :] = _dot(xb, win_ref[:, :M_INNER])
    op_ref[0] = _dot(xb, win_ref[:, M_INNER:]).astype(op_ref.dtype)
    gates = bg_ref[...]
    for gi in range(M_INNER // MXU_DIM):
        cs = slice(gi * MXU_DIM, (gi + 1) * MXU_DIM)
        conv = cb_ref[:, cs]
        for j in range(M_CONV):
            off = SUBLANES - (M_CONV - 1) + j
            conv = conv + cw_ref[j:j + 1, cs] * ext_s[off:off + tm, cs]
        x_c = conv * jax.nn.sigmoid(conv)
        xcb = x_c.astype(BF16)
        xc_ref[0, :, cs] = xcb
        xmb = ext_s[SUBLANES:, cs].astype(BF16)
        for src, w_ref, dst, part in ((xcb, wq_ref, q_ref, 0), (xcb, wk_ref, k_ref, 1), (xmb, wv_ref, v_ref, 2)):
            y = _dot(src, w_ref[gi]).astype(BF16)
            dst[0, :, cs] = y
            gates = gates + _dot(y, wg_ref[part * M_INNER + gi * MXU_DIM:part * M_INNER + (gi + 1) * MXU_DIM, :])
    ext_s[0:SUBLANES, :] = ext_s[tm:tm + SUBLANES, :]
    gate_ref[0] = gates[:, :2 * M_HEADS]


def _mlstm_pre(xb, win, cw, cb, wq, wk, wv, wg, bg, *, tm):
    b, s, d = xb.shape
    const2 = lambda bi, t: (0, 0)
    const3 = lambda bi, t: (0, 0, 0)
    tile = lambda dt: jax.ShapeDtypeStruct((b, s, M_INNER), dt)
    ospec = pl.BlockSpec((1, tm, M_INNER), lambda bi, t: (bi, t, 0))
    return pl.pallas_call(
        functools.partial(_mlstm_pre_kernel, tm=tm),
        grid=(b, s // tm),
        in_specs=[pl.BlockSpec((1, tm, d), lambda bi, t: (bi, t, 0)),
                  pl.BlockSpec(win.shape, const2, pipeline_mode=pl.Buffered(1)),
                  pl.BlockSpec(cw.shape, const2),
                  pl.BlockSpec(cb.shape, const2),
                  pl.BlockSpec(wq.shape, const3),
                  pl.BlockSpec(wk.shape, const3),
                  pl.BlockSpec(wv.shape, const3),
                  pl.BlockSpec(wg.shape, const2),
                  pl.BlockSpec(bg.shape, const2)],
        out_specs=[ospec, ospec, ospec, ospec, ospec,
                   pl.BlockSpec((1, tm, 2 * M_HEADS), lambda bi, t: (bi, t, 0))],
        out_shape=[tile(BF16), tile(BF16), tile(BF16), tile(BF16), tile(BF16),
                   jax.ShapeDtypeStruct((b, s, 2 * M_HEADS), F32)],
        scratch_shapes=[pltpu.VMEM((tm + SUBLANES, M_INNER), F32)],
        compiler_params=_cparams(("parallel", "arbitrary")),
        name="mlstm_pre",
    )(xb, win, cw, cb, wq, wk, wv, wg, bg)


def _mlstm_kernel(q_ref, k_ref, v_ref, xc_ref, op_ref, gc_ref, gr_ref, skip_ref, ng_ref, tri_ref, ones_ref,
                  y_ref, c_s, n_s, m_s, num_s, rden_s, *, chunk):
    @pl.when(pl.program_id(1) == 0)
    def _init():
        c_s[...] = jnp.zeros_like(c_s)
        n_s[...] = jnp.zeros_like(n_s)
        m_s[...] = jnp.zeros_like(m_s)

    tri = tri_ref[...]
    t_idx = lax.broadcasted_iota(I32, (chunk, chunk), 0)
    s_idx = lax.broadcasted_iota(I32, (chunk, chunk), 1)
    causal = s_idx <= t_idx
    gc = gc_ref[0]
    gr = gr_ref[0]
    for h in range(M_HEADS):
        cs = slice(h * M_DH, (h + 1) * M_DH)
        li_c = gc[:, h:h + 1]
        lf_c = _log_sigmoid(gc[:, M_HEADS + h:M_HEADS + h + 1])
        li_r = gr[h:h + 1, :]
        lf_r = _log_sigmoid(gr[M_HEADS + h:M_HEADS + h + 1, :])
        bc2 = _dot(tri, _split_bf16(jnp.broadcast_to(lf_c, (chunk, LANES))))
        b_c = (bc2[:, :LANES] + bc2[:, LANES:])[:, 0:1]
        lf_r8 = jnp.broadcast_to(lf_r, (SUBLANES, chunk))
        hi = lf_r8.astype(BF16)
        lo = (lf_r8 - hi.astype(F32)).astype(BF16)
        br2 = _dot_nt(jnp.concatenate([hi, lo], axis=0), tri)
        b_r = br2[0:1, :] + br2[SUBLANES:SUBLANES + 1, :]

        m_prev = m_s[h][0:1, 0:1]
        dmat = jnp.where(causal, b_c - b_r + li_r, -jnp.inf)
        inter = b_c + m_prev
        m_t = jnp.maximum(inter, jnp.max(dmat, axis=1, keepdims=True))
        w_inter = jnp.exp(inter - m_t)

        qb = q_ref[0, :, cs]
        kb = (k_ref[0, :, cs].astype(F32) * M_DH ** -0.5).astype(BF16)
        vb = v_ref[0, :, cs]
        n_prev = n_s[h][0:1, :]

        pb = (jnp.exp(dmat - m_t) * _dot_nt(qb, kb)).astype(BF16)
        num_s[...] = w_inter * _dot(qb, c_s[h].astype(BF16)) + _dot(pb, vb)
        qn = _dot_nt(qb, n_s[h].astype(BF16))[:, 0:1]
        den = w_inter * qn + _dot(pb, ones_ref[...])[:, 0:1]
        rden_s[...] = jnp.broadcast_to(1.0 / jnp.maximum(jnp.abs(den), jnp.exp(-m_t)), rden_s.shape)

        g_tot = b_c[chunk - 1:chunk, :]
        d_end = g_tot - b_c + li_c
        m_new = jnp.maximum(g_tot + m_prev, jnp.max(d_end, axis=0, keepdims=True))
        wk = jnp.exp(d_end - m_new)
        decay = jnp.exp(g_tot + m_prev - m_new)
        kw = k_ref[0, :, cs].astype(F32) * (wk * M_DH ** -0.5)
        kwb = kw.astype(BF16)
        n_s[h] = jnp.broadcast_to(decay * n_prev + jnp.sum(kw, axis=0, keepdims=True), n_s.shape[1:])
        m_s[h] = jnp.broadcast_to(m_new, m_s.shape[1:])
        for r in range(M_DH // LANES):
            rs = slice(r * LANES, (r + 1) * LANES)
            c_s[h, rs, :] = decay * c_s[h, rs, :] + _dot_tn(kwb[:, rs], vb)

        slab = min(chunk, 64)
        for r in range(chunk // slab):
            rs = slice(r * slab, (r + 1) * slab)
            hg = num_s[rs, :] * rden_s[rs, 0:1] * jax.nn.sigmoid(op_ref[0, rs, cs].astype(F32))
            mu = jnp.mean(hg, axis=-1, keepdims=True)
            hc = hg - mu
            var = jnp.mean(hc * hc, axis=-1, keepdims=True)
            hn = hc * lax.rsqrt(var + LN_EPS) * ng_ref[h:h + 1, :]
            y_ref[0, rs, cs] = (hn + skip_ref[:, cs] * xc_ref[0, rs, cs].astype(F32)).astype(y_ref.dtype)


def _mlstm(q, k, v, xc, op, gates, skip, norm_g, *, chunk=MLSTM_CHUNK):
    b, s, _ = q.shape
    gates_t = jnp.swapaxes(gates, 1, 2)
    tri = jnp.asarray(np.tril(np.ones((chunk, chunk), np.float32)), BF16)
    ones = jnp.ones((chunk, LANES), BF16)
    big = pl.BlockSpec((1, chunk, M_INNER), lambda bi, c: (bi, c, 0))
    const2 = lambda bi, c: (0, 0)
    return pl.pallas_call(
        functools.partial(_mlstm_kernel, chunk=chunk),
        grid=(b, s // chunk),
        in_specs=[big, big, big, big, big,
                  pl.BlockSpec((1, chunk, 2 * M_HEADS), lambda bi, c: (bi, c, 0)),
                  pl.BlockSpec((1, 2 * M_HEADS, chunk), lambda bi, c: (bi, 0, c)),
                  pl.BlockSpec(skip.shape, const2),
                  pl.BlockSpec(norm_g.shape, const2),
                  pl.BlockSpec(tri.shape, const2),
                  pl.BlockSpec(ones.shape, const2)],
        out_specs=big,
        out_shape=jax.ShapeDtypeStruct((b, s, M_INNER), BF16),
        scratch_shapes=[pltpu.VMEM((M_HEADS, M_DH, M_DH), F32),
                        pltpu.VMEM((M_HEADS, SUBLANES, M_DH), F32),
                        pltpu.VMEM((M_HEADS, SUBLANES, LANES), F32),
                        pltpu.VMEM((chunk, M_DH), F32),
                        pltpu.VMEM((chunk, LANES), F32)],
        compiler_params=_cparams(("parallel", "arbitrary")),
        name="mlstm",
    )(q, k, v, xc, op, gates, gates_t, skip, norm_g, tri, ones)


def _even_w_in(w):
    sizes = (512, 64, 64, 512, 64, 8, 512, 512, 512, 512, 16)
    offs = np.concatenate([[0], np.cumsum(sizes)])
    part = lambda j: w[:, offs[j]:offs[j + 1]]
    a_q, a_k, a_v, i_q, i_k, i_w, g_q, g_k, g_v, g_r, g_lr = (part(j) for j in range(11))
    zeros = lambda n: jnp.zeros((w.shape[0], n), w.dtype)
    cols = [a_q, i_q, g_q, g_k, g_v, g_r, a_k, a_v, i_k, i_w, zeros(LANES - 72), g_lr, zeros(LANES - G_RANK)]
    return jnp.concatenate(cols, axis=1).astype(BF16)


def _block_diag(w):
    per = MXU_DIM // M_QKV_BLOCK
    wg = w.reshape(-1, per, M_QKV_BLOCK, M_QKV_BLOCK)
    eye = jnp.eye(per, dtype=w.dtype)
    dense = jnp.einsum('gade,ab->gadbe', wg, eye)
    return dense.reshape(-1, MXU_DIM, MXU_DIM).astype(BF16)


def _token_tile(t):
    for tm in (512, 256, 128):
        if t % tm == 0:
            return tm
    raise ValueError(f"token count {t} is not a multiple of 128")


def kernel(x, ev_w_in, ev_g_w2, ev_g_b2, ev_g_norm, ev_w_out, od_w_in, od_conv_w, od_conv_b, od_w_q, od_w_k, od_w_v, od_w_gate, od_b_gate, od_skip, od_norm, od_w_out, ln1_g, ln1_b, ln2_g, ln2_b, mlp_w1, mlp_w2):
    b, s, d = x.shape
    t = b * s
    tm = _token_tile(t)
    xf = x.reshape(t, d)
    xb = xf.astype(BF16)
    row = lambda v: v.reshape(1, -1)
    a_width = A_HEADS * A_HEAD_DIM
    for l in range(DEPTH):
        j = l // 2
        if l % 2 == 0:
            proj = _matmul(xb, _even_w_in(ev_w_in[j]), tm=tm, tn=EV_COLS_PADDED // 3, out_dtype=BF16)
            proj = proj.reshape(b, s, EV_COLS_PADDED)
            a_out = _dsa(proj, nbatch=next(n for n in (4, 2, 1) if b % n == 0))
            g_out = _gla(proj, ev_g_w2[j].astype(BF16), row(ev_g_b2[j]), ev_g_norm[j])
            w_out = ev_w_out[j].astype(BF16)
            acts = [a_out.reshape(t, a_width), g_out.reshape(t, -1)]
            weights = [w_out[:a_width], w_out[a_width:]]
        else:
            wg = jnp.pad(od_w_gate[j], ((0, 0), (0, LANES - 2 * M_HEADS))).astype(BF16)
            bg = jnp.pad(od_b_gate[j], (0, LANES - 2 * M_HEADS)).reshape(1, LANES)
            q, k, v, xc, op, gates = _mlstm_pre(
                xb.reshape(b, s, d), od_w_in[j].astype(BF16), od_conv_w[j], row(od_conv_b[j]),
                _block_diag(od_w_q[j]), _block_diag(od_w_k[j]), _block_diag(od_w_v[j]), wg, bg,
                tm=min(256, s))
            y = _mlstm(q, k, v, xc, op, gates, row(od_skip[j]), od_norm[j], chunk=min(MLSTM_CHUNK, s))
            acts = [y.reshape(t, M_INNER)]
            weights = [od_w_out[j].astype(BF16)]
        xf, xb = _layer_tail(acts, weights, xf, row(ln1_g[l]), row(ln1_b[l]),
                             mlp_w1[l].astype(BF16), mlp_w2[l].astype(BF16), row(ln2_g[l]), row(ln2_b[l]), tm=tm)
    return xf.reshape(b, s, d)
```
